```python
import jax, jax.numpy as jnp
from jax import lax
import numpy as np

D_MODEL = 1024
BATCH = 8
SEQ = 8192
DEPTH = 1

D_MIX = D_MODEL
D_MLSTM = D_MIX // 2
D_MOBA = D_MIX - D_MLSTM
MLSTM_HEADS = 4
MLSTM_HEAD_DIM = D_MLSTM // MLSTM_HEADS
MLSTM_CHUNK = 64
CONV_WIDTH = 4
MOBA_HEADS = 8
MOBA_HEAD_DIM = D_MOBA // MOBA_HEADS
MOBA_BLOCK = 256
MOBA_TOPK = 3
MOBA_Q_CHUNK = 32
N_GROUPS = 4
EXPERTS_PER_GROUP = 8
N_EXPERTS = N_GROUPS * EXPERTS_PER_GROUP
EXPERT_TOPK = 2
D_EXPERT = 512
MOE_ROW_BLOCK = 256
EPS = 1e-6
IN_COLS = 4 * D_MLSTM + 2 * MLSTM_HEADS + 3 * D_MOBA

kernel_name = 'hymba_mlstm_moba_hmoe_adaln'


def rms_norm(x, g):
    xf = x.astype(jnp.float32)
    y = xf * lax.rsqrt(jnp.mean(xf * xf, axis=-1, keepdims=True) + EPS)
    return (y * g.astype(jnp.float32)).astype(x.dtype)


def split_heads(a, n_heads):
    b, s, _ = a.shape
    return a.reshape(b, s, n_heads, -1).transpose(0, 2, 1, 3)


def merge_heads(a):
    b, h, s, d = a.shape
    return a.transpose(0, 2, 1, 3).reshape(b, s, h * d)


def causal_depthwise_conv(x, w, b):
    s = x.shape[1]
    xp = jnp.pad(x, ((0, 0), (CONV_WIDTH - 1, 0), (0, 0)))
    y = b
    for j in range(CONV_WIDTH):
        y = y + w[j] * xp[:, j:j + s]
    return y


def mlstm_chunkwise(q, k, v, i_pre, f_pre):
    b, h, s, d = q.shape
    dv = v.shape[-1]
    nc = s // MLSTM_CHUNK
    out_dtype = v.dtype
    qf = q.astype(jnp.float32) * (d ** -0.5)
    kf = k.astype(jnp.float32)
    vf = v.astype(jnp.float32)
    log_i = i_pre.astype(jnp.float32)
    log_f = jax.nn.log_sigmoid(f_pre.astype(jnp.float32))

    def to_chunks(a):
        a = a.reshape(b, h, nc, MLSTM_CHUNK, *a.shape[3:])
        return jnp.moveaxis(a, 2, 0)

    causal = jnp.tril(jnp.ones((MLSTM_CHUNK, MLSTM_CHUNK), dtype=bool))

    def step(carry, inp):
        C, n, m = carry
        qt, kt, vt, it, ft = inp
        cum_f = jnp.cumsum(ft, axis=-1)
        cum_last = cum_f[..., -1]
        dmat = cum_f[..., :, None] - cum_f[..., None, :] + it[..., None, :]
        dmat = jnp.where(causal, dmat, -jnp.inf)
        inter = cum_f + m[..., None]
        m_t = jnp.maximum(inter, jnp.max(dmat, axis=-1))
        w_inter = jnp.exp(inter - m_t)
        sc = jnp.einsum('bhtd,bhsd->bhts', qt, kt) * jnp.exp(dmat - m_t[..., None])
        num = w_inter[..., None] * jnp.einsum('bhtd,bhde->bhte', qt, C) + jnp.einsum('bhts,bhse->bhte', sc, vt)
        den = w_inter * jnp.einsum('bhtd,bhd->bht', qt, n) + jnp.sum(sc, axis=-1)
        h_t = num / jnp.maximum(jnp.abs(den), jnp.exp(-m_t))[..., None]
        g_end = cum_last[..., None] - cum_f + it
        m_new = jnp.maximum(cum_last + m, jnp.max(g_end, axis=-1))
        w_old = jnp.exp(cum_last + m - m_new)
        w_new = jnp.exp(g_end - m_new[..., None])
        C_new = w_old[..., None, None] * C + jnp.einsum('bhs,bhsd,bhse->bhde', w_new, kt, vt)
        n_new = w_old[..., None] * n + jnp.einsum('bhs,bhsd->bhd', w_new, kt)
        return (C_new, n_new, m_new), h_t

    init = (jnp.zeros((b, h, d, dv), jnp.float32), jnp.zeros((b, h, d), jnp.float32),
            jnp.zeros((b, h), jnp.float32))
    _, hs = lax.scan(step, init, (to_chunks(qf), to_chunks(kf), to_chunks(vf),
                                  to_chunks(log_i), to_chunks(log_f)))
    return jnp.moveaxis(hs, 0, 2).reshape(b, h, s, dv).astype(out_dtype)


def moba_attention(q, k, v):
    b, h, s, d = q.shape
    nb = -(-s // MOBA_BLOCK)
    pad = nb * MOBA_BLOCK - s
    kp = jnp.pad(k.astype(jnp.float32), ((0, 0), (0, 0), (0, pad), (0, 0)))
    vp = jnp.pad(v.astype(jnp.float32), ((0, 0), (0, 0), (0, pad), (0, 0)))
    kb = kp.reshape(b, h, nb, MOBA_BLOCK, d)
    vb = vp.reshape(b, h, nb, MOBA_BLOCK, d)
    k_mean = jnp.mean(kb, axis=3)
    scale = d ** -0.5
    n_sel = min(MOBA_TOPK, nb - 1)
    bi = jnp.arange(b)[:, None, None, None]
    hi = jnp.arange(h)[None, :, None, None]
    blk_ids = jnp.arange(nb)

    def chunk(ci):
        t0 = ci * MOBA_Q_CHUNK
        qc = lax.dynamic_slice_in_dim(q, t0, MOBA_Q_CHUNK, axis=2).astype(jnp.float32)
        q_pos = t0 + jnp.arange(MOBA_Q_CHUNK)
        own = t0 // MOBA_BLOCK
        k_own = lax.dynamic_index_in_dim(kb, own, axis=2, keepdims=False)
        v_own = lax.dynamic_index_in_dim(vb, own, axis=2, keepdims=False)
        own_pos = own * MOBA_BLOCK + jnp.arange(MOBA_BLOCK)
        s_own = jnp.einsum('bhqd,bhkd->bhqk', qc, k_own) * scale
        s_own = jnp.where(own_pos[None, :] <= q_pos[:, None], s_own, -jnp.inf)
        if n_sel > 0:
            gate = jnp.einsum('bhqd,bhnd->bhqn', qc, k_mean)
            gate = jnp.where(blk_ids < own, gate, -jnp.inf)
            _, sel = lax.top_k(gate, n_sel)
            valid = sel < own
            k_sel = kb[bi, hi, sel]
            v_sel = vb[bi, hi, sel]
            s_sel = jnp.einsum('bhqd,bhqskd->bhqsk', qc, k_sel) * scale
            s_sel = jnp.where(valid[..., None], s_sel, -jnp.inf)
            p = jax.nn.softmax(jnp.concatenate(
                [s_sel.reshape(b, h, MOBA_Q_CHUNK, n_sel * MOBA_BLOCK), s_own], axis=-1), axis=-1)
            p_sel = p[..., :n_sel * MOBA_BLOCK].reshape(b, h, MOBA_Q_CHUNK, n_sel, MOBA_BLOCK)
            p_own = p[..., n_sel * MOBA_BLOCK:]
            return (jnp.einsum('bhqsk,bhqskd->bhqd', p_sel, v_sel)
                    + jnp.einsum('bhqk,bhkd->bhqd', p_own, v_own))
        p_own = jax.nn.softmax(s_own, axis=-1)
        return jnp.einsum('bhqk,bhkd->bhqd', p_own, v_own)

    outs = lax.map(chunk, jnp.arange(s // MOBA_Q_CHUNK))
    return jnp.moveaxis(outs, 0, 2).reshape(b, h, s, d).astype(q.dtype)


def hierarchical_moe(hx, w_rg, b_rg, w_re, b_re, w_gate, w_up, w_down):
    b, s, dm = hx.shape
    t = b * s
    xt = hx.reshape(t, dm)
    g_prob = jax.nn.softmax((xt @ w_rg).astype(jnp.float32) + b_rg, axis=-1)
    g_w, g_idx = lax.top_k(g_prob, 1)
    e_logits = ((xt @ w_re).astype(jnp.float32) + b_re).reshape(t, N_GROUPS, EXPERTS_PER_GROUP)
    e_logits = jnp.take_along_axis(e_logits, g_idx[:, :, None], axis=1)[:, 0]
    e_top, e_loc = lax.top_k(e_logits, EXPERT_TOPK)
    e_w = jax.nn.softmax(e_top, axis=-1) * g_w
    e_id = g_idx * EXPERTS_PER_GROUP + e_loc

    n_assign = t * EXPERT_TOPK
    flat_e = e_id.reshape(n_assign)
    flat_tok = jnp.repeat(jnp.arange(t), EXPERT_TOPK)
    flat_w = e_w.reshape(n_assign)
    order = jnp.argsort(flat_e)
    se, stok, sw = flat_e[order], flat_tok[order], flat_w[order]
    counts = jnp.zeros((N_EXPERTS,), jnp.int32).at[flat_e].add(1)
    starts = jnp.cumsum(counts) - counts
    padded = (counts + MOE_ROW_BLOCK - 1) // MOE_ROW_BLOCK * MOE_ROW_BLOCK
    pad_ends = jnp.cumsum(padded)
    pad_starts = pad_ends - padded
    dest = pad_starts[se] + (jnp.arange(n_assign) - starts[se])
    n_blocks = -(-n_assign // MOE_ROW_BLOCK) + N_EXPERTS
    rows = n_blocks * MOE_ROW_BLOCK
    x_pad = jnp.zeros((rows, dm), hx.dtype).at[dest].set(xt[stok])
    blk_expert = jnp.minimum(
        jnp.searchsorted(pad_ends, jnp.arange(n_blocks) * MOE_ROW_BLOCK, side='right'), N_EXPERTS - 1)

    def run_block(args):
        xb, e = args
        return (jax.nn.silu(xb @ w_gate[e]) * (xb @ w_up[e])) @ w_down[e]

    y_pad = lax.map(run_block, (x_pad.reshape(n_blocks, MOE_ROW_BLOCK, dm), blk_expert))
    y_sorted = y_pad.reshape(rows, dm)[dest].astype(jnp.float32)
    out = jnp.zeros((t, dm), jnp.float32).at[stok].add(y_sorted * sw[:, None])
    return out.reshape(b, s, dm).astype(hx.dtype)


def setup_inputs(seed: int = 0) -> dict:
    key = jax.random.key(seed)
    ks = jax.random.split(key, 24)
    f32 = jnp.float32

    def nrm(k, shape, sc):
        return sc * jax.random.normal(k, shape, f32)

    L = DEPTH
    x = nrm(ks[0], (BATCH, SEQ, D_MODEL), 1.0)
    c = nrm(ks[1], (BATCH, D_MODEL), 1.0)
    w_ada = nrm(ks[2], (L, D_MODEL, 6 * D_MODEL), 0.5 * D_MODEL ** -0.5)
    b_ada = nrm(ks[3], (L, 6 * D_MODEL), 0.02)
    g_norm1 = 1.0 + nrm(ks[4], (L, D_MODEL), 0.02)
    w_in = nrm(ks[5], (L, D_MODEL, IN_COLS), D_MODEL ** -0.5)
    w_conv = nrm(ks[6], (L, CONV_WIDTH, 2 * D_MLSTM), CONV_WIDTH ** -0.5)
    b_conv = nrm(ks[7], (L, 2 * D_MLSTM), 0.02)
    b_i = nrm(ks[8], (L, MLSTM_HEADS), 0.1)
    b_f = jnp.linspace(3.0, 6.0, MLSTM_HEADS, dtype=f32) + nrm(ks[9], (L, MLSTM_HEADS), 0.1)
    b_gates = jnp.concatenate([b_i, b_f], axis=-1)
    g_mlstm_head = 1.0 + nrm(ks[10], (L, D_MLSTM), 0.02)
    w_out = nrm(ks[11], (L, D_MIX, D_MODEL), D_MIX ** -0.5)
    g_norm2 = 1.0 + nrm(ks[12], (L, D_MODEL), 0.02)
    w_router_group = nrm(ks[13], (L, D_MODEL, N_GROUPS), D_MODEL ** -0.5)
    b_router_group = nrm(ks[14], (L, N_GROUPS), 0.01)
    w_router_expert = nrm(ks[15], (L, D_MODEL, N_EXPERTS), D_MODEL ** -0.5)
    b_router_expert = nrm(ks[16], (L, N_EXPERTS), 0.01)
    w_expert_gate = nrm(ks[17], (L, N_EXPERTS, D_MODEL, D_EXPERT), D_MODEL ** -0.5)
    w_expert_up = nrm(ks[18], (L, N_EXPERTS, D_MODEL, D_EXPERT), D_MODEL ** -0.5)
    w_expert_down = nrm(ks[19], (L, N_EXPERTS, D_EXPERT, D_MODEL), D_EXPERT ** -0.5)
    g_final = 1.0 + nrm(ks[20], (D_MODEL,), 0.02)
    return {'x': x, 'c': c, 'w_ada': w_ada, 'b_ada': b_ada, 'g_norm1': g_norm1, 'w_in': w_in,
            'w_conv': w_conv, 'b_conv': b_conv, 'b_gates': b_gates, 'g_mlstm_head': g_mlstm_head,
            'w_out': w_out, 'g_norm2': g_norm2, 'w_router_group': w_router_group,
            'b_router_group': b_router_group, 'w_router_expert': w_router_expert,
            'b_router_expert': b_router_expert, 'w_expert_gate': w_expert_gate,
            'w_expert_up': w_expert_up, 'w_expert_down': w_expert_down, 'g_final': g_final}


def reference(x, c, w_ada, b_ada, g_norm1, w_in, w_conv, b_conv, b_gates, g_mlstm_head,
              w_out, g_norm2, w_router_group, b_router_group, w_router_expert,
              b_router_expert, w_expert_gate, w_expert_up, w_expert_down, g_final):
    split_at = np.cumsum([D_MLSTM, D_MLSTM, D_MLSTM, D_MLSTM, 2 * MLSTM_HEADS, D_MOBA, D_MOBA]).tolist()
    for l in range(DEPTH):
        mod = jax.nn.silu(c) @ w_ada[l] + b_ada[l]
        shift1, scale1, gate1, shift2, scale2, gate2 = [m[:, None, :] for m in jnp.split(mod, 6, axis=-1)]

        hn = rms_norm(x, g_norm1[l]) * (1.0 + scale1) + shift1
        proj = hn @ w_in[l]
        q_m, k_m, v_m, o_m, gates, q_a, k_a, v_a = jnp.split(proj, split_at, axis=-1)

        qk = jax.nn.silu(causal_depthwise_conv(jnp.concatenate([q_m, k_m], axis=-1), w_conv[l], b_conv[l]))
        q_m, k_m = jnp.split(qk, 2, axis=-1)
        gates = gates + b_gates[l]
        i_pre = gates[..., :MLSTM_HEADS].transpose(0, 2, 1)
        f_pre = gates[..., MLSTM_HEADS:].transpose(0, 2, 1)
        h_m = mlstm_chunkwise(split_heads(q_m, MLSTM_HEADS), split_heads(k_m, MLSTM_HEADS),
                              split_heads(v_m, MLSTM_HEADS), i_pre, f_pre)
        h_m = rms_norm(h_m, g_mlstm_head[l].reshape(MLSTM_HEADS, 1, MLSTM_HEAD_DIM))
        h_m = merge_heads(h_m) * jax.nn.sigmoid(o_m)

        h_a = merge_heads(moba_attention(split_heads(q_a, MOBA_HEADS), split_heads(k_a, MOBA_HEADS),
                                         split_heads(v_a, MOBA_HEADS)))

        mix = jnp.concatenate([h_m, h_a], axis=-1) @ w_out[l]
        x = x + gate1 * mix

        hn2 = rms_norm(x, g_norm2[l]) * (1.0 + scale2) + shift2
        ffn = hierarchical_moe(hn2, w_router_group[l], b_router_group[l], w_router_expert[l],
                               b_router_expert[l], w_expert_gate[l], w_expert_up[l], w_expert_down[l])
        x = x + gate2 * ffn
    return rms_norm(x, g_final)
```

```python
import functools

import jax
import jax.numpy as jnp
from jax import lax
from jax.experimental import pallas as pl
from jax.experimental.pallas import tpu as pltpu

F32 = jnp.float32
BF16 = jnp.bfloat16

MLSTM_HEADS = 4
MLSTM_HEAD_DIM = 128
CONV_WIDTH = 4
MOBA_HEADS = 8
MOBA_HEAD_DIM = 64
MOBA_BLOCK = 256
MOBA_TOPK = 3
N_GROUPS = 4
EXPERTS_PER_GROUP = 8
N_EXPERTS = N_GROUPS * EXPERTS_PER_GROUP
EPS = 1e-6

LANES = 128
SUBLANES = 8
ROW_BLOCK = 256
MLSTM_CHUNK = 256
TOKEN_TILE = 512
COMBINE_TILE = 256
NEG = -1e30
VMEM_LIMIT = 48 * 1024 * 1024


def _dot(a, b):
    return jnp.dot(a, b, preferred_element_type=F32)


def _dot_nt(a, b):
    return lax.dot_general(a, b, (((1,), (1,)), ((), ())), preferred_element_type=F32)


def _dot_tn(a, b):
    return lax.dot_general(a, b, (((0,), (0,)), ((), ())), preferred_element_type=F32)


def _sigmoid(x):
    return 1.0 / (1.0 + jnp.exp(-x))


def _log_sigmoid(x):
    return jnp.minimum(x, 0.0) - jnp.log1p(jnp.exp(-jnp.abs(x)))


def _split3(x):
    hi = x.astype(BF16)
    r1 = x - hi.astype(F32)
    mid = r1.astype(BF16)
    lo = (r1 - mid.astype(F32)).astype(BF16)
    return hi, mid, lo


def _iota(shape, dim):
    return lax.broadcasted_iota(jnp.int32, shape, dim)


def _adaln_kernel(c_ref, w_ref, b_ref, o_ref):
    c = c_ref[...]
    o_ref[...] = _dot(c * _sigmoid(c), w_ref[...]) + b_ref[...]


def _adaln(c, w, b):
    bsz, d = c.shape
    n = w.shape[1]
    tn = 1024
    return pl.pallas_call(
        _adaln_kernel,
        out_shape=jax.ShapeDtypeStruct((bsz, n), F32),
        grid=(n // tn,),
        in_specs=[pl.BlockSpec((bsz, d), lambda j: (0, 0)),
                  pl.BlockSpec((d, tn), lambda j: (0, j)),
                  pl.BlockSpec((1, tn), lambda j: (0, j))],
        out_specs=pl.BlockSpec((bsz, tn), lambda j: (0, j)),
        name="adaln",
    )(c, w, b.reshape(1, n))


def _inproj_kernel(x_ref, shift_ref, scale_ref, g_ref, wm_ref, wa_ref, wg_ref, wgt_ref, bg_ref, bgt_ref,
                   pm_ref, pa_ref, gc_ref, gt_ref):
    x = x_ref[...]
    ms = jnp.mean(x * x, axis=-1, keepdims=True)
    y = x * lax.rsqrt(ms + EPS) * g_ref[...]
    hb = (y * (1.0 + scale_ref[...]) + shift_ref[...]).astype(BF16)
    cw = 512
    for j in range(wm_ref.shape[1] // cw):
        pm_ref[:, j * cw:(j + 1) * cw] = _dot(hb, wm_ref[:, j * cw:(j + 1) * cw])
    for j in range(wa_ref.shape[1] // cw):
        pa_ref[:, j * cw:(j + 1) * cw] = _dot(hb, wa_ref[:, j * cw:(j + 1) * cw]).astype(BF16)
    gc_ref[...] = _dot(hb, wg_ref[...]) + bg_ref[...]
    gt_ref[...] = _dot_nt(wgt_ref[...], hb) + bgt_ref[...]


def _inproj(x, mod, g1, wm, wa, wg, wgt, bg, bgt):
    bsz, s, d = x.shape
    tm = min(TOKEN_TILE, s)
    nm, na = wm.shape[1], wa.shape[1]
    ng = wgt.shape[0]
    const = lambda b, i: (0, 0)
    return pl.pallas_call(
        _inproj_kernel,
        out_shape=(jax.ShapeDtypeStruct((bsz, s, nm), F32),
                   jax.ShapeDtypeStruct((bsz, s, na), BF16),
                   jax.ShapeDtypeStruct((bsz, s, LANES), F32),
                   jax.ShapeDtypeStruct((bsz, ng, s), F32)),
        grid=(bsz, s // tm),
        in_specs=[pl.BlockSpec((None, tm, d), lambda b, i: (b, i, 0)),
                  pl.BlockSpec((None, None, 1, d), lambda b, i: (b, 0, 0, 0)),
                  pl.BlockSpec((None, None, 1, d), lambda b, i: (b, 1, 0, 0)),
                  pl.BlockSpec((1, d), const),
                  pl.BlockSpec((d, nm), const),
                  pl.BlockSpec((d, na), const),
                  pl.BlockSpec((d, LANES), const),
                  pl.BlockSpec((ng, d), const),
                  pl.BlockSpec((1, LANES), const),
                  pl.BlockSpec((ng, 1), const)],
        out_specs=(pl.BlockSpec((None, tm, nm), lambda b, i: (b, i, 0)),
                   pl.BlockSpec((None, tm, na), lambda b, i: (b, i, 0)),
                   pl.BlockSpec((None, tm, LANES), lambda b, i: (b, i, 0)),
                   pl.BlockSpec((None, ng, tm), lambda b, i: (b, 0, i))),
        compiler_params=pltpu.CompilerParams(dimension_semantics=("arbitrary", "arbitrary"),
                                             vmem_limit_bytes=VMEM_LIMIT),
        name="inproj",
    )(x, mod, mod, g1, wm, wa, wg, wgt, bg, bgt)


def _mlstm_kernel(q_ref, k_ref, v_ref, o_ref, gc_ref, gt_ref, wc_ref, bc_ref, gh_ref, out_ref,
                  qbuf, kbuf, c_sc, n_sc, m_sc):
    L = q_ref.shape[0]
    H, DH = MLSTM_HEADS, MLSTM_HEAD_DIM
    HD = H * DH
    ci = pl.program_id(1)

    @pl.when(ci == 0)
    def _():
        qbuf[0:SUBLANES, :] = jnp.zeros((SUBLANES, HD), F32)
        kbuf[0:SUBLANES, :] = jnp.zeros((SUBLANES, HD), F32)
        c_sc[...] = jnp.zeros_like(c_sc)
        n_sc[...] = jnp.zeros_like(n_sc)
        m_sc[...] = jnp.zeros_like(m_sc)

    qbuf[SUBLANES:SUBLANES + L, :] = q_ref[...]
    kbuf[SUBLANES:SUBLANES + L, :] = k_ref[...]
    qc = jnp.broadcast_to(bc_ref[:, 0:HD], (L, HD))
    kc = jnp.broadcast_to(bc_ref[:, HD:2 * HD], (L, HD))
    for j in range(CONV_WIDTH):
        off = SUBLANES - (CONV_WIDTH - 1) + j
        qc = qc + wc_ref[j:j + 1, 0:HD] * qbuf[off:off + L, :]
        kc = kc + wc_ref[j:j + 1, HD:2 * HD] * kbuf[off:off + L, :]
    qbuf[0:SUBLANES, :] = qbuf[L:L + SUBLANES, :]
    kbuf[0:SUBLANES, :] = kbuf[L:L + SUBLANES, :]
    qa = qc * _sigmoid(qc) * (DH ** -0.5)
    ka = kc * _sigmoid(kc)

    gc = gc_ref[...]
    gt = gt_ref[...]
    row = _iota((L, L), 0)
    col = _iota((L, L), 1)
    causal = col <= row
    tril = causal.astype(BF16)
    triu = (row <= col).astype(BF16)
    h3 = _split3(_log_sigmoid(gc))
    cum_c = _dot(tril, h3[0]) + _dot(tril, h3[1]) + _dot(tril, h3[2])
    r3 = _split3(_log_sigmoid(gt))
    cum_r = _dot(r3[0], triu) + _dot(r3[1], triu) + _dot(r3[2], triu)

    for h in range(H):
        sl = slice(h * DH, (h + 1) * DH)
        qf = qa[:, sl]
        qb = qf.astype(BF16)
        kf = ka[:, sl]
        kb = kf.astype(BF16)
        vb = v_ref[:, sl].astype(BF16)
        a_c = cum_c[:, H + h:H + h + 1]
        li_c = gc[:, h:h + 1]
        b_r = gt[h:h + 1, :] - cum_r[H + h:H + h + 1, :]
        cum_last = cum_r[H + h:H + h + 1, L - 1:L]
        m_prev = m_sc[h:h + 1, 0:1]
        c_prev = c_sc[h]
        n_prev = n_sc[h:h + 1, :]

        dm = jnp.where(causal, a_c + b_r, -jnp.inf)
        inter = a_c + m_prev
        m_t = jnp.maximum(inter, jnp.max(dm, axis=-1, keepdims=True))
        w_inter = jnp.exp(inter - m_t)
        sc = _dot_nt(qb, kb) * jnp.exp(dm - m_t)
        num = w_inter * _dot(qb, c_prev.astype(BF16)) + _dot(sc.astype(BF16), vb)
        den = w_inter * jnp.sum(qf * n_prev, axis=-1, keepdims=True) + jnp.sum(sc, axis=-1, keepdims=True)
        ht = num / jnp.maximum(jnp.abs(den), jnp.exp(-m_t))

        g_end_r = cum_last + b_r
        m_new = jnp.maximum(cum_last + m_prev, jnp.max(g_end_r, axis=-1, keepdims=True))
        w_old = jnp.exp(cum_last + m_prev - m_new)
        w_new_c = jnp.exp(cum_last + li_c - a_c - m_new)
        kw = kf * w_new_c
        c_sc[h] = w_old * c_prev + _dot_tn(kw.astype(BF16), vb)
        n_sc[h:h + 1, :] = w_old * n_prev + jnp.sum(kw, axis=0, keepdims=True)
        m_sc[h:h + 1, :] = jnp.broadcast_to(m_new, (1, LANES))

        hn = ht * lax.rsqrt(jnp.mean(ht * ht, axis=-1, keepdims=True) + EPS) * gh_ref[:, sl]
        out_ref[:, sl] = (hn * _sigmoid(o_ref[:, sl])).astype(BF16)


def _mlstm(pm, gc, gt, w_conv, b_conv, g_head):
    bsz, s, _ = pm.shape
    L = min(MLSTM_CHUNK, s)
    H, DH = MLSTM_HEADS, MLSTM_HEAD_DIM
    HD = H * DH
    const = lambda b, i: (0, 0)
    col_block = lambda j: pl.BlockSpec((None, L, HD), lambda b, i: (b, i, j))
    return pl.pallas_call(
        _mlstm_kernel,
        out_shape=jax.ShapeDtypeStruct((bsz, s, HD), BF16),
        grid=(bsz, s // L),
        in_specs=[col_block(0), col_block(1), col_block(2), col_block(3),
                  pl.BlockSpec((None, L, LANES), lambda b, i: (b, i, 0)),
                  pl.BlockSpec((None, 2 * H, L), lambda b, i: (b, 0, i)),
                  pl.BlockSpec((CONV_WIDTH, 2 * HD), const),
                  pl.BlockSpec((1, 2 * HD), const),
                  pl.BlockSpec((1, HD), const)],
        out_specs=pl.BlockSpec((None, L, HD), lambda b, i: (b, i, 0)),
        scratch_shapes=[pltpu.VMEM((L + SUBLANES, HD), F32),
                        pltpu.VMEM((L + SUBLANES, HD), F32),
                        pltpu.VMEM((H, DH, DH), F32),
                        pltpu.VMEM((SUBLANES, DH), F32),
                        pltpu.VMEM((SUBLANES, LANES), F32)],
        compiler_params=pltpu.CompilerParams(dimension_semantics=("arbitrary", "arbitrary"),
                                             vmem_limit_bytes=VMEM_LIMIT),
        name="mlstm",
    )(pm, pm, pm, pm, gc, gt, w_conv, b_conv.reshape(1, 2 * HD), g_head.reshape(1, HD))


def _moba_kernel(q_ref, k_ref, v_ref, out_ref, km_sc):
    BS = MOBA_BLOCK
    DHA = MOBA_HEAD_DIM
    nb = k_ref.shape[0] // BS
    qi = pl.program_id(2)

    @pl.when(qi == 0)
    def _():
        def mean_body(j, carry):
            kj = k_ref[pl.ds(pl.multiple_of(j * BS, BS), BS), :].astype(F32)
            km_sc[pl.ds(j, 1), :] = jnp.mean(kj, axis=0, keepdims=True)
            return carry
        lax.fori_loop(0, nb, mean_body, 0)

    lane = _iota((BS, LANES), 1)
    first = lane < DHA
    q = q_ref[...].astype(F32) * (DHA ** -0.5)
    qh = (jnp.where(first, q, 0.0).astype(BF16), jnp.where(first, 0.0, q).astype(BF16))

    kmb = km_sc[...].astype(BF16)
    blk = _iota((BS, nb), 1)
    blkf = blk.astype(F32)
    sel = []
    for h in range(2):
        g = jnp.where(blk < qi, _dot_nt(qh[h], kmb), -jnp.inf)
        picked = jnp.zeros((BS, nb), jnp.bool_)
        for _ in range(min(MOBA_TOPK, nb - 1)):
            mx = jnp.max(g, axis=-1, keepdims=True)
            idx = jnp.min(jnp.where(g == mx, blkf, float(nb)), axis=-1, keepdims=True)
            pick = blkf == idx
            picked = jnp.logical_or(picked, pick)
            g = jnp.where(pick, -jnp.inf, g)
        sel.append(jnp.where(jnp.logical_and(picked, blk < qi), 1.0, 0.0))

    def head_mask(a0, a1):
        return jnp.where(first, a0, a1)

    def attend(kj, vj, masks, state):
        m_old, l_old, acc = state
        m_new, l_new, alpha, pv = [], [], [], []
        for h in range(2):
            s = jnp.where(masks[h], _dot_nt(qh[h], kj), NEG)
            mn = jnp.maximum(m_old[h], jnp.max(s, axis=-1, keepdims=True))
            p = jnp.exp(s - mn)
            a = jnp.exp(m_old[h] - mn)
            m_new.append(mn)
            alpha.append(a)
            l_new.append(a * l_old[h] + jnp.sum(p, axis=-1, keepdims=True))
            pv.append(_dot(p.astype(BF16), vj))
        return (tuple(m_new), tuple(l_new), acc * head_mask(alpha[0], alpha[1]) + head_mask(pv[0], pv[1]))

    row = _iota((BS, BS), 0)
    col = _iota((BS, BS), 1)
    diag = col <= row
    own = pl.ds(pl.multiple_of(qi * BS, BS), BS)
    init = ((jnp.full((BS, 1), NEG, F32),) * 2, (jnp.zeros((BS, 1), F32),) * 2, jnp.zeros((BS, LANES), F32))
    state = attend(k_ref[own, :], v_ref[own, :], (diag, diag), init)

    def past_body(j, state):
        rows = pl.ds(pl.multiple_of(j * BS, BS), BS)
        masks = tuple(jnp.sum(jnp.where(blk == j, sel[h], 0.0), axis=-1, keepdims=True) > 0.5 for h in range(2))
        return attend(k_ref[rows, :], v_ref[rows, :], masks, state)

    _, l_fin, acc = lax.fori_loop(0, qi, past_body, state)
    out_ref[...] = (acc / head_mask(l_fin[0], l_fin[1])).astype(BF16)


def _moba(pa):
    bsz, s, _ = pa.shape
    BS = MOBA_BLOCK
    npair = MOBA_HEADS * MOBA_HEAD_DIM // LANES
    nb = s // BS
    return pl.pallas_call(
        _moba_kernel,
        out_shape=jax.ShapeDtypeStruct((bsz, s, npair * LANES), BF16),
        grid=(bsz, npair, nb),
        in_specs=[pl.BlockSpec((None, BS, LANES), lambda b, p, i: (b, i, p)),
                  pl.BlockSpec((None, s, LANES), lambda b, p, i: (b, 0, npair + p)),
                  pl.BlockSpec((None, s, LANES), lambda b, p, i: (b, 0, 2 * npair + p))],
        out_specs=pl.BlockSpec((None, BS, LANES), lambda b, p, i: (b, i, p)),
        scratch_shapes=[pltpu.VMEM((nb, LANES), F32)],
        compiler_params=pltpu.CompilerParams(dimension_semantics=("arbitrary",) * 3,
                                             vmem_limit_bytes=VMEM_LIMIT),
        name="moba",
    )(pa, pa, pa)


def _outproj_kernel(hm_ref, ha_ref, x_ref, gate_ref, shift_ref, scale_ref, g_ref, wo_ref, wr_ref, br_ref,
                    x1_ref, hn_ref, ri_ref):
    dm = hm_ref.shape[1]
    mix = _dot(hm_ref[...], wo_ref[0:dm, :]) + _dot(ha_ref[...], wo_ref[dm:, :])
    x1 = x_ref[...] + gate_ref[...] * mix
    x1_ref[...] = x1
    ms = jnp.mean(x1 * x1, axis=-1, keepdims=True)
    hn = x1 * lax.rsqrt(ms + EPS) * g_ref[...] * (1.0 + scale_ref[...]) + shift_ref[...]
    hn_ref[...] = hn
    logits = _dot(hn.astype(BF16), wr_ref[...]) + br_ref[...]

    tm = logits.shape[0]
    lane = _iota((tm, LANES), 1)
    lanef = lane.astype(F32)
    big = float(LANES)
    in_groups = lane < N_GROUPS
    gl = jnp.where(in_groups, logits, -jnp.inf)
    gmax = jnp.max(gl, axis=-1, keepdims=True)
    gidx = jnp.min(jnp.where(gl == gmax, lanef, big), axis=-1, keepdims=True)
    g_w = 1.0 / jnp.sum(jnp.where(in_groups, jnp.exp(gl - gmax), 0.0), axis=-1, keepdims=True)
    lo = N_GROUPS + EXPERTS_PER_GROUP * gidx
    el = jnp.where(jnp.logical_and(lanef >= lo, lanef < lo + EXPERTS_PER_GROUP), logits, -jnp.inf)
    e1 = jnp.max(el, axis=-1, keepdims=True)
    i1 = jnp.min(jnp.where(el == e1, lanef, big), axis=-1, keepdims=True)
    el = jnp.where(lanef == i1, -jnp.inf, el)
    e2 = jnp.max(el, axis=-1, keepdims=True)
    i2 = jnp.min(jnp.where(el == e2, lanef, big), axis=-1, keepdims=True)
    t = jnp.exp(e2 - e1)
    w1 = g_w / (1.0 + t)
    w2 = g_w * t / (1.0 + t)
    ri = jnp.where(lane == 0, i1 - N_GROUPS,
                   jnp.where(lane == 1, i2 - N_GROUPS,
                             jnp.where(lane == 2, w1, jnp.where(lane == 3, w2, 0.0))))
    ri_ref[...] = ri


def _outproj(hm, ha, x, mod, g2, wo, wr, br):
    bsz, s, d = x.shape
    tm = min(TOKEN_TILE, s)
    dh = hm.shape[2]
    const = lambda b, i: (0, 0)
    tile = lambda w: pl.BlockSpec((None, tm, w), lambda b, i: (b, i, 0))
    modrow = lambda k: pl.BlockSpec((None, None, 1, d), lambda b, i: (b, k, 0, 0))
    return pl.pallas_call(
        _outproj_kernel,
        out_shape=(jax.ShapeDtypeStruct((bsz, s, d), F32),
                   jax.ShapeDtypeStruct((bsz, s, d), F32),
                   jax.ShapeDtypeStruct((bsz, s, LANES), F32)),
        grid=(bsz, s // tm),
        in_specs=[tile(dh), tile(dh), tile(d), modrow(2), modrow(3), modrow(4),
                  pl.BlockSpec((1, d), const),
                  pl.BlockSpec((2 * dh, d), const),
                  pl.BlockSpec((d, LANES), const),
                  pl.BlockSpec((1, LANES), const)],
        out_specs=(tile(d), tile(d), tile(LANES)),
        compiler_params=pltpu.CompilerParams(dimension_semantics=("arbitrary", "arbitrary"),
                                             vmem_limit_bytes=VMEM_LIMIT),
        name="outproj",
    )(hm, ha, x, mod, mod, mod, g2, wo, wr, br)


def _rank_kernel(ri_ref, dest_ref, cnt_ref, cnt_sc, run_sc):
    ph = pl.program_id(0)
    i = pl.program_id(1)
    tm = ri_ref.shape[0]
    lane = _iota((tm, LANES), 1).astype(F32)
    ri = ri_ref[...]
    oh0 = lane == ri[:, 0:1]
    oh1 = lane == ri[:, 1:2]
    oh = jnp.where(jnp.logical_or(oh0, oh1), 1.0, 0.0)
    colsum = jnp.sum(oh, axis=0, keepdims=True)

    @pl.when(jnp.logical_and(ph == 0, i == 0))
    def _():
        cnt_sc[...] = jnp.zeros_like(cnt_sc)

    @pl.when(ph == 0)
    def _():
        cnt_sc[...] = cnt_sc[...] + colsum
        dest_ref[...] = jnp.zeros_like(dest_ref)
        cnt_ref[...] = cnt_sc[...]

    @pl.when(jnp.logical_and(ph == 1, i == 0))
    def _():
        padded = jnp.ceil(cnt_sc[...] / ROW_BLOCK) * ROW_BLOCK
        l8 = _iota((SUBLANES, LANES), 1)
        acc = padded
        k = 1
        while k < LANES:
            acc = acc + jnp.where(l8 >= k, pltpu.roll(acc, k, 1), 0.0)
            k *= 2
        run_sc[...] = acc - padded

    @pl.when(ph == 1)
    def _():
        row = _iota((tm, tm), 0)
        col = _iota((tm, tm), 1)
        before = _dot((col < row).astype(BF16), oh.astype(BF16))
        tot = before + run_sc[0:1, :]
        d0 = jnp.sum(jnp.where(oh0, tot, 0.0), axis=-1, keepdims=True)
        d1 = jnp.sum(jnp.where(oh1, tot, 0.0), axis=-1, keepdims=True)
        lane_i = _iota((tm, LANES), 1)
        dest_ref[...] = jnp.where(lane_i == 0, d0, jnp.where(lane_i == 1, d1, 0.0))
        run_sc[...] = run_sc[...] + colsum
        cnt_ref[...] = cnt_sc[...]


def _rank(ri):
    t = ri.shape[0]
    tm = min(TOKEN_TILE, t)
    return pl.pallas_call(
        _rank_kernel,
        out_shape=(jax.ShapeDtypeStruct((t, LANES), F32),
                   jax.ShapeDtypeStruct((SUBLANES, LANES), F32)),
        grid=(2, t // tm),
        in_specs=[pl.BlockSpec((tm, LANES), lambda p, i: (i, 0))],
        out_specs=(pl.BlockSpec((tm, LANES), lambda p, i: (i * p, 0)),
                   pl.BlockSpec((SUBLANES, LANES), lambda p, i: (0, 0))),
        scratch_shapes=[pltpu.VMEM((SUBLANES, LANES), F32), pltpu.VMEM((SUBLANES, LANES), F32)],
        compiler_params=pltpu.CompilerParams(dimension_semantics=("arbitrary", "arbitrary")),
        name="rank",
    )(ri)


def _dispatch_kernel(dest_ref, hn_ref, xpad_in_ref, xpad_ref, sem):
    del xpad_in_ref
    tm = hn_ref.shape[0]

    def row_copy(t, k):
        d = dest_ref[0, 2 * t + k]
        return pltpu.make_async_copy(hn_ref.at[pl.ds(t, 1), :], xpad_ref.at[pl.ds(d, 1), :], sem)

    def issue(t, carry):
        row_copy(t, 0).start()
        row_copy(t, 1).start()
        return carry
    lax.fori_loop(0, tm, issue, 0)

    def drain(t, carry):
        row_copy(t, 0).wait()
        row_copy(t, 1).wait()
        return carry
    lax.fori_loop(0, tm, drain, 0)


def _dispatch(dest, hn, rows):
    t, d = hn.shape
    tm = min(TOKEN_TILE, t)
    xpad0 = jnp.zeros((rows, d), hn.dtype)
    return pl.pallas_call(
        _dispatch_kernel,
        out_shape=jax.ShapeDtypeStruct((rows, d), hn.dtype),
        grid=(t // tm,),
        in_specs=[pl.BlockSpec((None, 1, 2 * tm), lambda i: (i, 0, 0), memory_space=pltpu.SMEM),
                  pl.BlockSpec((tm, d), lambda i: (i, 0)),
                  pl.BlockSpec(memory_space=pl.ANY)],
        out_specs=pl.BlockSpec(memory_space=pl.ANY),
        scratch_shapes=[pltpu.SemaphoreType.DMA],
        input_output_aliases={2: 0},
        compiler_params=pltpu.CompilerParams(dimension_semantics=("arbitrary",)),
        name="dispatch",
    )(dest.reshape(t // tm, 1, 2 * tm), hn, xpad0)


def _expert_kernel(be_ref, nu_ref, x_ref, wg_ref, wu_ref, wd_ref, y_ref):
    i = pl.program_id(0)

    @pl.when(i < nu_ref[0])
    def _():
        xb = x_ref[...].astype(BF16)
        g = _dot(xb, wg_ref[...])
        u = _dot(xb, wu_ref[...])
        y_ref[...] = _dot((g * _sigmoid(g) * u).astype(BF16), wd_ref[...])

    @pl.when(i >= nu_ref[0])
    def _():
        y_ref[...] = jnp.zeros_like(y_ref)


def _experts(blk_expert, n_used, xpad, wg, wu, wd):
    rows, d = xpad.shape
    de = wg.shape[2]
    nblk = rows // ROW_BLOCK
    live = lambda i, be, nu: jnp.minimum(i, nu[0] - 1)
    return pl.pallas_call(
        _expert_kernel,
        out_shape=jax.ShapeDtypeStruct((rows, d), F32),
        grid_spec=pltpu.PrefetchScalarGridSpec(
            num_scalar_prefetch=2,
            grid=(nblk,),
            in_specs=[pl.BlockSpec((ROW_BLOCK, d), lambda i, be, nu: (live(i, be, nu), 0)),
                      pl.BlockSpec((None, d, de), lambda i, be, nu: (be[live(i, be, nu)], 0, 0)),
                      pl.BlockSpec((None, d, de), lambda i, be, nu: (be[live(i, be, nu)], 0, 0)),
                      pl.BlockSpec((None, de, d), lambda i, be, nu: (be[live(i, be, nu)], 0, 0))],
            out_specs=pl.BlockSpec((ROW_BLOCK, d), lambda i, be, nu: (i, 0))),
        compiler_params=pltpu.CompilerParams(dimension_semantics=("arbitrary",),
                                             vmem_limit_bytes=VMEM_LIMIT),
        name="experts",
    )(blk_expert, n_used, xpad, wg, wu, wd)


def _combine_kernel(final_norm, dest_ref, ypad_ref, x1_ref, ri_ref, gate_ref, gf_ref, out_ref, ybuf, sem):
    tm = x1_ref.shape[0]

    def row_copy(t, k):
        d = dest_ref[0, 2 * t + k]
        return pltpu.make_async_copy(ypad_ref.at[pl.ds(d, 1), :], ybuf.at[k, pl.ds(t, 1), :], sem)

    def issue(t, carry):
        row_copy(t, 0).start()
        row_copy(t, 1).start()
        return carry
    lax.fori_loop(0, tm, issue, 0)

    def drain(t, carry):
        row_copy(t, 0).wait()
        row_copy(t, 1).wait()
        return carry
    lax.fori_loop(0, tm, drain, 0)

    ri = ri_ref[...]
    ffn = ri[:, 2:3] * ybuf[0] + ri[:, 3:4] * ybuf[1]
    x2 = x1_ref[...] + gate_ref[...] * ffn
    if final_norm:
        ms = jnp.mean(x2 * x2, axis=-1, keepdims=True)
        x2 = x2 * lax.rsqrt(ms + EPS) * gf_ref[...]
    out_ref[...] = x2


def _combine(dest, ypad, x1, ri, mod, gf, final_norm):
    bsz, s, d = x1.shape
    tm = min(COMBINE_TILE, s)
    nt = s // tm
    return pl.pallas_call(
        functools.partial(_combine_kernel, final_norm),
        out_shape=jax.ShapeDtypeStruct((bsz, s, d), F32),
        grid=(bsz, nt),
        in_specs=[pl.BlockSpec((None, 1, 2 * tm), lambda b, i: (b * nt + i, 0, 0), memory_space=pltpu.SMEM),
                  pl.BlockSpec(memory_space=pl.ANY),
                  pl.BlockSpec((None, tm, d), lambda b, i: (b, i, 0)),
                  pl.BlockSpec((None, tm, LANES), lambda b, i: (b, i, 0)),
                  pl.BlockSpec((None, None, 1, d), lambda b, i: (b, 5, 0, 0)),
                  pl.BlockSpec((1, d), lambda b, i: (0, 0))],
        out_specs=pl.BlockSpec((None, tm, d), lambda b, i: (b, i, 0)),
        scratch_shapes=[pltpu.VMEM((2, tm, d), F32), pltpu.SemaphoreType.DMA],
        compiler_params=pltpu.CompilerParams(dimension_semantics=("arbitrary", "arbitrary"),
                                             vmem_limit_bytes=VMEM_LIMIT),
        name="combine",
    )(dest.reshape(bsz * nt, 1, 2 * tm), ypad, x1, ri.reshape(bsz, s, LANES), mod, gf)


def _pad_lanes(a):
    return jnp.pad(a, ((0, 0), (0, LANES - a.shape[1])))


def kernel(x, c, w_ada, b_ada, g_norm1, w_in, w_conv, b_conv, b_gates, g_mlstm_head, w_out, g_norm2,
           w_router_group, b_router_group, w_router_expert, b_router_expert, w_expert_gate, w_expert_up,
           w_expert_down, g_final):
    bsz, s, d = x.shape
    t = bsz * s
    dm = MLSTM_HEADS * MLSTM_HEAD_DIM
    da = MOBA_HEADS * MOBA_HEAD_DIM
    ng = 2 * MLSTM_HEADS
    n_assign = 2 * t
    n_blocks = -(-n_assign // ROW_BLOCK) + N_EXPERTS
    rows = n_blocks * ROW_BLOCK

    for l in range(w_ada.shape[0]):
        mod = _adaln(c, w_ada[l], b_ada[l]).reshape(bsz, 6, 1, d)

        w = w_in[l]
        wm = w[:, :4 * dm].astype(BF16)
        wgates = w[:, 4 * dm:4 * dm + ng]
        wa = w[:, 4 * dm + ng:].astype(BF16)
        pm, pa, gc, gt = _inproj(x, mod, g_norm1[l].reshape(1, d), wm, wa,
                                 _pad_lanes(wgates).astype(BF16), wgates.T.astype(BF16),
                                 _pad_lanes(b_gates[l].reshape(1, ng)), b_gates[l].reshape(ng, 1))
        hm = _mlstm(pm, gc, gt, w_conv[l], b_conv[l], g_mlstm_head[l])
        ha = _moba(pa)

        wr = _pad_lanes(jnp.concatenate([w_router_group[l], w_router_expert[l]], axis=1)).astype(BF16)
        br = _pad_lanes(jnp.concatenate([b_router_group[l], b_router_expert[l]]).reshape(1, -1))
        x1, hn2, ri = _outproj(hm, ha, x, mod, g_norm2[l].reshape(1, d), w_out[l].astype(BF16), wr, br)
        ri = ri.reshape(t, LANES)
        destf, cnt = _rank(ri)
        dest = destf[:, :2].astype(jnp.int32).reshape(n_assign)
        counts = cnt[0, :N_EXPERTS].astype(jnp.int32)
        pad_ends = jnp.cumsum((counts + ROW_BLOCK - 1) // ROW_BLOCK)
        blk_expert = jnp.minimum(
            jnp.searchsorted(pad_ends, jnp.arange(n_blocks, dtype=jnp.int32), side='right'),
            N_EXPERTS - 1).astype(jnp.int32)
        n_used = pad_ends[-1:].astype(jnp.int32)
        xpad = _dispatch(dest, hn2.reshape(t, d), rows)
        ypad = _experts(blk_expert, n_used, xpad, w_expert_gate[l].astype(BF16),
                        w_expert_up[l].astype(BF16), w_expert_down[l].astype(BF16))
        x = _combine(dest, ypad, x1, ri, mod, g_final.reshape(1, d), l == w_ada.shape[0] - 1)
    return x
```

```python
import functools

import jax
import jax.numpy as jnp
from jax import lax
from jax.experimental import pallas as pl
from jax.experimental.pallas import tpu as pltpu

F32 = jnp.float32
BF16 = jnp.bfloat16

MLSTM_HEADS = 4
MLSTM_HEAD_DIM = 128
CONV_WIDTH = 4
MOBA_HEADS = 8
MOBA_HEAD_DIM = 64
MOBA_BLOCK = 256
MOBA_TOPK = 3
N_GROUPS = 4
EXPERTS_PER_GROUP = 8
N_EXPERTS = N_GROUPS * EXPERTS_PER_GROUP
EPS = 1e-6

LANES = 128
SUBLANES = 8
ROW_BLOCK = 256
MLSTM_CHUNK = 256
TOKEN_TILE = 512
COMBINE_TILE = 256
ROW_DMA_UNROLL = 8
NEG = -1e30
LOG2E = 1.4426950408889634
MOBA_GROUP = 2
MOBA_GROUP_KEYS = MOBA_GROUP * MOBA_BLOCK
VMEM_LIMIT = 48 * 1024 * 1024


def _dot(a, b):
    return jnp.dot(a, b, preferred_element_type=F32)


def _dot_nt(a, b):
    return lax.dot_general(a, b, (((1,), (1,)), ((), ())), preferred_element_type=F32)


def _dot_tn(a, b):
    return lax.dot_general(a, b, (((0,), (0,)), ((), ())), preferred_element_type=F32)


def _sigmoid(x):
    return 1.0 / (1.0 + jnp.exp(-x))


def _log_sigmoid(x):
    return jnp.minimum(x, 0.0) - jnp.log1p(jnp.exp(-jnp.abs(x)))


def _split3(x):
    hi = x.astype(BF16)
    r1 = x - hi.astype(F32)
    mid = r1.astype(BF16)
    lo = (r1 - mid.astype(F32)).astype(BF16)
    return hi, mid, lo


def _iota(shape, dim):
    return lax.broadcasted_iota(jnp.int32, shape, dim)


def _adaln_kernel(c_ref, w_ref, b_ref, o_ref):
    c = c_ref[...]
    o_ref[...] = _dot(c * _sigmoid(c), w_ref[...]) + b_ref[...]


def _adaln(c, w, b):
    bsz, d = c.shape
    n = w.shape[1]
    tn = 1024
    return pl.pallas_call(
        _adaln_kernel,
        out_shape=jax.ShapeDtypeStruct((bsz, n), F32),
        grid=(n // tn,),
        in_specs=[pl.BlockSpec((bsz, d), lambda j: (0, 0)),
                  pl.BlockSpec((d, tn), lambda j: (0, j)),
                  pl.BlockSpec((1, tn), lambda j: (0, j))],
        out_specs=pl.BlockSpec((bsz, tn), lambda j: (0, j)),
        name="adaln",
    )(c, w, b.reshape(1, n))


def _inproj_kernel(x_ref, shift_ref, scale_ref, g_ref, wm_ref, wa_ref, wvt_ref, wg_ref, wgt_ref, bg_ref, bgt_ref,
                   pm_ref, pa_ref, vt_ref, gc_ref, gt_ref):
    x = x_ref[...]
    ms = jnp.mean(x * x, axis=-1, keepdims=True)
    y = x * lax.rsqrt(ms + EPS) * g_ref[...]
    hb = (y * (1.0 + scale_ref[...]) + shift_ref[...]).astype(BF16)
    cw = 512
    for j in range(wm_ref.shape[1] // cw):
        pm_ref[:, j * cw:(j + 1) * cw] = _dot(hb, wm_ref[:, j * cw:(j + 1) * cw])
    for j in range(wa_ref.shape[1] // cw):
        pa_ref[:, j * cw:(j + 1) * cw] = _dot(hb, wa_ref[:, j * cw:(j + 1) * cw]).astype(BF16)
    vt = _dot_nt(wvt_ref[...], hb).astype(BF16)
    for j in range(vt_ref.shape[0]):
        vt_ref[j] = vt[:, j * MOBA_GROUP_KEYS:(j + 1) * MOBA_GROUP_KEYS]
    gc_ref[...] = _dot(hb, wg_ref[...]) + bg_ref[...]
    gt_ref[...] = _dot_nt(wgt_ref[...], hb) + bgt_ref[...]


def _inproj(x, mod, g1, wm, wa, wvt, wg, wgt, bg, bgt):
    bsz, s, d = x.shape
    tm = min(TOKEN_TILE, s)
    nm, na, nv = wm.shape[1], wa.shape[1], wvt.shape[0]
    vb = tm // MOBA_GROUP_KEYS
    ng = wgt.shape[0]
    const = lambda b, i: (0, 0)
    return pl.pallas_call(
        _inproj_kernel,
        out_shape=(jax.ShapeDtypeStruct((bsz, s, nm), F32),
                   jax.ShapeDtypeStruct((bsz, s, na), BF16),
                   jax.ShapeDtypeStruct((bsz, s // MOBA_GROUP_KEYS, nv, MOBA_GROUP_KEYS), BF16),
                   jax.ShapeDtypeStruct((bsz, s, LANES), F32),
                   jax.ShapeDtypeStruct((bsz, ng, s), F32)),
        grid=(bsz, s // tm),
        in_specs=[pl.BlockSpec((None, tm, d), lambda b, i: (b, i, 0)),
                  pl.BlockSpec((None, None, 1, d), lambda b, i: (b, 0, 0, 0)),
                  pl.BlockSpec((None, None, 1, d), lambda b, i: (b, 1, 0, 0)),
                  pl.BlockSpec((1, d), const),
                  pl.BlockSpec((d, nm), const),
                  pl.BlockSpec((d, na), const),
                  pl.BlockSpec((nv, d), const),
                  pl.BlockSpec((d, LANES), const),
                  pl.BlockSpec((ng, d), const),
                  pl.BlockSpec((1, LANES), const),
                  pl.BlockSpec((ng, 1), const)],
        out_specs=(pl.BlockSpec((None, tm, nm), lambda b, i: (b, i, 0)),
                   pl.BlockSpec((None, tm, na), lambda b, i: (b, i, 0)),
                   pl.BlockSpec((None, vb, nv, MOBA_GROUP_KEYS), lambda b, i: (b, i, 0, 0)),
                   pl.BlockSpec((None, tm, LANES), lambda b, i: (b, i, 0)),
                   pl.BlockSpec((None, ng, tm), lambda b, i: (b, 0, i))),
        compiler_params=pltpu.CompilerParams(dimension_semantics=("arbitrary", "arbitrary"),
                                             vmem_limit_bytes=VMEM_LIMIT),
        name="inproj",
    )(x, mod, mod, g1, wm, wa, wvt, wg, wgt, bg, bgt)


def _mlstm_kernel(q_ref, k_ref, v_ref, o_ref, gc_ref, gt_ref, wc_ref, bc_ref, gh_ref, out_ref,
                  qbuf, kbuf, c_sc, n_sc, m_sc):
    L = q_ref.shape[0]
    H, DH = MLSTM_HEADS, MLSTM_HEAD_DIM
    HD = H * DH
    ci = pl.program_id(1)

    @pl.when(ci == 0)
    def _():
        qbuf[0:SUBLANES, :] = jnp.zeros((SUBLANES, HD), F32)
        kbuf[0:SUBLANES, :] = jnp.zeros((SUBLANES, HD), F32)
        c_sc[...] = jnp.zeros_like(c_sc)
        n_sc[...] = jnp.zeros_like(n_sc)
        m_sc[...] = jnp.zeros_like(m_sc)

    qbuf[SUBLANES:SUBLANES + L, :] = q_ref[...]
    kbuf[SUBLANES:SUBLANES + L, :] = k_ref[...]
    qc = jnp.broadcast_to(bc_ref[:, 0:HD], (L, HD))
    kc = jnp.broadcast_to(bc_ref[:, HD:2 * HD], (L, HD))
    for j in range(CONV_WIDTH):
        off = SUBLANES - (CONV_WIDTH - 1) + j
        qc = qc + wc_ref[j:j + 1, 0:HD] * qbuf[off:off + L, :]
        kc = kc + wc_ref[j:j + 1, HD:2 * HD] * kbuf[off:off + L, :]
    qbuf[0:SUBLANES, :] = qbuf[L:L + SUBLANES, :]
    kbuf[0:SUBLANES, :] = kbuf[L:L + SUBLANES, :]
    qa = qc * _sigmoid(qc) * (DH ** -0.5)
    ka = kc * _sigmoid(kc)

    gc = gc_ref[...]
    gt = gt_ref[...]
    row = _iota((L, L), 0)
    col = _iota((L, L), 1)
    causal = col <= row
    tril = causal.astype(BF16)
    triu = (row <= col).astype(BF16)
    h3 = _split3(_log_sigmoid(gc))
    cum_c = _dot(tril, h3[0]) + _dot(tril, h3[1]) + _dot(tril, h3[2])
    r3 = _split3(_log_sigmoid(gt))
    cum_r = _dot(r3[0], triu) + _dot(r3[1], triu) + _dot(r3[2], triu)

    for h in range(H):
        sl = slice(h * DH, (h + 1) * DH)
        qf = qa[:, sl]
        qb = qf.astype(BF16)
        kf = ka[:, sl]
        kb = kf.astype(BF16)
        vb = v_ref[:, sl].astype(BF16)
        a_c = cum_c[:, H + h:H + h + 1]
        li_c = gc[:, h:h + 1]
        b_r = gt[h:h + 1, :] - cum_r[H + h:H + h + 1, :]
        cum_last = cum_r[H + h:H + h + 1, L - 1:L]
        m_prev = m_sc[h:h + 1, 0:1]
        c_prev = c_sc[h]
        n_prev = n_sc[h:h + 1, :]

        dm = jnp.where(causal, a_c + b_r, -jnp.inf)
        inter = a_c + m_prev
        m_t = jnp.maximum(inter, jnp.max(dm, axis=-1, keepdims=True))
        w_inter = jnp.exp(inter - m_t)
        sc = _dot_nt(qb, kb) * jnp.exp(dm - m_t)
        num = w_inter * _dot(qb, c_prev.astype(BF16)) + _dot(sc.astype(BF16), vb)
        den = w_inter * jnp.sum(qf * n_prev, axis=-1, keepdims=True) + jnp.sum(sc, axis=-1, keepdims=True)
        ht = num / jnp.maximum(jnp.abs(den), jnp.exp(-m_t))

        g_end_r = cum_last + b_r
        m_new = jnp.maximum(cum_last + m_prev, jnp.max(g_end_r, axis=-1, keepdims=True))
        w_old = jnp.exp(cum_last + m_prev - m_new)
        w_new_c = jnp.exp(cum_last + li_c - a_c - m_new)
        kw = kf * w_new_c
        c_sc[h] = w_old * c_prev + _dot_tn(kw.astype(BF16), vb)
        n_sc[h:h + 1, :] = w_old * n_prev + jnp.sum(kw, axis=0, keepdims=True)
        m_sc[h:h + 1, :] = jnp.broadcast_to(m_new, (1, LANES))

        hn = ht * lax.rsqrt(jnp.mean(ht * ht, axis=-1, keepdims=True) + EPS) * gh_ref[:, sl]
        out_ref[:, sl] = (hn * _sigmoid(o_ref[:, sl])).astype(BF16)


def _mlstm(pm, gc, gt, w_conv, b_conv, g_head):
    bsz, s, _ = pm.shape
    L = min(MLSTM_CHUNK, s)
    H, DH = MLSTM_HEADS, MLSTM_HEAD_DIM
    HD = H * DH
    const = lambda b, i: (0, 0)
    col_block = lambda j: pl.BlockSpec((None, L, HD), lambda b, i: (b, i, j))
    return pl.pallas_call(
        _mlstm_kernel,
        out_shape=jax.ShapeDtypeStruct((bsz, s, HD), BF16),
        grid=(bsz, s // L),
        in_specs=[col_block(0), col_block(1), col_block(2), col_block(3),
                  pl.BlockSpec((None, L, LANES), lambda b, i: (b, i, 0)),
                  pl.BlockSpec((None, 2 * H, L), lambda b, i: (b, 0, i)),
                  pl.BlockSpec((CONV_WIDTH, 2 * HD), const),
                  pl.BlockSpec((1, 2 * HD), const),
                  pl.BlockSpec((1, HD), const)],
        out_specs=pl.BlockSpec((None, L, HD), lambda b, i: (b, i, 0)),
        scratch_shapes=[pltpu.VMEM((L + SUBLANES, HD), F32),
                        pltpu.VMEM((L + SUBLANES, HD), F32),
                        pltpu.VMEM((H, DH, DH), F32),
                        pltpu.VMEM((SUBLANES, DH), F32),
                        pltpu.VMEM((SUBLANES, LANES), F32)],
        compiler_params=pltpu.CompilerParams(dimension_semantics=("arbitrary", "arbitrary"),
                                             vmem_limit_bytes=VMEM_LIMIT),
        name="mlstm",
    )(pm, pm, pm, pm, gc, gt, w_conv, b_conv.reshape(1, 2 * HD), g_head.reshape(1, HD))


def _moba_kernel(q_ref, k_ref, vt_ref, out_ref, km_sc, bias_sc, sa_sc, sb_sc):
    BS = MOBA_BLOCK
    DHA = MOBA_HEAD_DIM
    GB = MOBA_GROUP
    GK = GB * BS
    nb = k_ref.shape[0] // BS
    ngroups = nb // GB
    qi = pl.program_id(2)

    @pl.when(qi == 0)
    def _():
        def mean_body(j, carry):
            kj = k_ref[pl.ds(pl.multiple_of(j * BS, BS), BS), :].astype(F32)
            km_sc[pl.ds(j, 1), :] = jnp.mean(kj, axis=0, keepdims=True)
            return carry
        lax.fori_loop(0, nb, mean_body, 0)

    lane = _iota((BS, LANES), 1)
    first = lane < DHA
    q = q_ref[...].astype(F32) * (DHA ** -0.5 * LOG2E)
    qh = (jnp.where(first, q, 0.0).astype(BF16), jnp.where(first, 0.0, q).astype(BF16))

    kmb = km_sc[...].astype(BF16)
    blk = _iota((nb, BS), 0)
    blkf = blk.astype(F32)
    for h in range(2):
        g = jnp.where(blk < qi, _dot_nt(kmb, qh[h]), -jnp.inf)
        picked = jnp.zeros((nb, BS), jnp.bool_)
        for _ in range(min(MOBA_TOPK, nb - 1)):
            mx = jnp.max(g, axis=0, keepdims=True)
            idx = jnp.min(jnp.where(g == mx, blkf, float(nb)), axis=0, keepdims=True)
            pick = blkf == idx
            picked = jnp.logical_or(picked, pick)
            g = jnp.where(pick, -jnp.inf, g)
        bias_sc[h, 0:nb] = jnp.where(jnp.logical_and(picked, blk < qi), 0.0, NEG)
        bias_sc[h, nb:nb + SUBLANES] = jnp.full((SUBLANES, BS), NEG, F32)

    ones = jnp.ones((2 * SUBLANES, GK), BF16)

    def values(g, h):
        return jnp.concatenate([vt_ref[g, h * DHA:(h + 1) * DHA, :], ones], axis=0)

    def bias_row(h, g, c):
        return bias_sc[h, pl.ds(GB * g + c, 1), :]

    def scores(g, h, dst):
        gk = jnp.minimum(g, ngroups - 1)
        mx = None
        for c in range(GB):
            rows = pl.ds(pl.multiple_of(gk * GK + c * BS, BS), BS)
            s = _dot_nt(k_ref[rows, :], qh[h])
            dst[h, c] = s
            mxc = jnp.max(s, axis=0, keepdims=True) + bias_row(h, g, c)
            mx = mxc if mx is None else jnp.maximum(mx, mxc)
        return mx

    def weights(g, h, src, mx, state):
        m_old, acc = state
        mn = jnp.maximum(m_old, mx)
        p = [jnp.exp2(src[h, c] - (mn - bias_row(h, g, c))).astype(BF16) for c in range(GB)]
        pv = _dot(values(jnp.minimum(g, ngroups - 1), h), jnp.concatenate(p, axis=0))
        return mn, acc * jnp.exp2(m_old - mn) + pv

    key_pos = _iota((BS, BS), 0)
    query_pos = _iota((BS, BS), 1)
    causal = key_pos <= query_pos
    own = pl.ds(pl.multiple_of(qi * BS, BS), BS)
    half = qi % GB
    state = []
    for h in range(2):
        s = jnp.where(causal, _dot_nt(k_ref[own, :], qh[h]), NEG)
        mn = jnp.max(s, axis=0, keepdims=True)
        p = jnp.exp2(s - mn).astype(BF16)
        pz = jnp.zeros_like(p)
        p2 = jnp.concatenate([jnp.where(half == c, p, pz) for c in range(GB)], axis=0)
        state.append((mn, _dot(values(qi // GB, h), p2)))

    mx_a = [scores(0, h, sa_sc) for h in range(2)]

    def past_step(t, carry):
        state, mx_a = carry
        mx_b = [scores(2 * t + 1, h, sb_sc) for h in range(2)]
        state = [weights(2 * t, h, sa_sc, mx_a[h], state[h]) for h in range(2)]
        mx_a = [scores(2 * t + 2, h, sa_sc) for h in range(2)]
        state = [weights(2 * t + 1, h, sb_sc, mx_b[h], state[h]) for h in range(2)]
        return state, mx_a

    state, _ = lax.fori_loop(0, (qi + (2 * GB - 1)) // (2 * GB), past_step, (state, mx_a))
    out_t = jnp.concatenate([state[h][1][0:DHA] / state[h][1][DHA:DHA + 1] for h in range(2)], axis=0)
    out_ref[...] = out_t.T.astype(BF16)


def _moba(pa, vt):
    bsz, s, _ = pa.shape
    BS = MOBA_BLOCK
    npair = MOBA_HEADS * MOBA_HEAD_DIM // LANES
    nb = s // BS
    gb = MOBA_GROUP
    assert nb % gb == 0 and gb <= SUBLANES
    return pl.pallas_call(
        _moba_kernel,
        out_shape=jax.ShapeDtypeStruct((bsz, s, npair * LANES), BF16),
        grid=(bsz, npair, nb),
        in_specs=[pl.BlockSpec((None, BS, LANES), lambda b, p, i: (b, i, p)),
                  pl.BlockSpec((None, s, LANES), lambda b, p, i: (b, 0, npair + p)),
                  pl.BlockSpec((None, nb // gb, LANES, gb * BS), lambda b, p, i: (b, 0, p, 0))],
        out_specs=pl.BlockSpec((None, BS, LANES), lambda b, p, i: (b, i, p)),
        scratch_shapes=[pltpu.VMEM((nb, LANES), F32), pltpu.VMEM((2, nb + SUBLANES, BS), F32),
                        pltpu.VMEM((2, gb, BS, BS), F32), pltpu.VMEM((2, gb, BS, BS), F32)],
        compiler_params=pltpu.CompilerParams(dimension_semantics=("arbitrary",) * 3,
                                             vmem_limit_bytes=VMEM_LIMIT),
        name="moba",
    )(pa, pa, vt)


def _outproj_kernel(hm_ref, ha_ref, x_ref, gate_ref, shift_ref, scale_ref, g_ref, wo_ref, wr_ref, br_ref,
                    x1_ref, hn_ref, ri_ref):
    dm = hm_ref.shape[1]
    mix = _dot(hm_ref[...], wo_ref[0:dm, :]) + _dot(ha_ref[...], wo_ref[dm:, :])
    x1 = x_ref[...] + gate_ref[...] * mix
    x1_ref[...] = x1
    ms = jnp.mean(x1 * x1, axis=-1, keepdims=True)
    hn = x1 * lax.rsqrt(ms + EPS) * g_ref[...] * (1.0 + scale_ref[...]) + shift_ref[...]
    hn_ref[...] = hn
    logits = _dot(hn.astype(BF16), wr_ref[...]) + br_ref[...]

    tm = logits.shape[0]
    lane = _iota((tm, LANES), 1)
    lanef = lane.astype(F32)
    big = float(LANES)
    in_groups = lane < N_GROUPS
    gl = jnp.where(in_groups, logits, -jnp.inf)
    gmax = jnp.max(gl, axis=-1, keepdims=True)
    gidx = jnp.min(jnp.where(gl == gmax, lanef, big), axis=-1, keepdims=True)
    g_w = 1.0 / jnp.sum(jnp.where(in_groups, jnp.exp(gl - gmax), 0.0), axis=-1, keepdims=True)
    lo = N_GROUPS + EXPERTS_PER_GROUP * gidx
    el = jnp.where(jnp.logical_and(lanef >= lo, lanef < lo + EXPERTS_PER_GROUP), logits, -jnp.inf)
    e1 = jnp.max(el, axis=-1, keepdims=True)
    i1 = jnp.min(jnp.where(el == e1, lanef, big), axis=-1, keepdims=True)
    el = jnp.where(lanef == i1, -jnp.inf, el)
    e2 = jnp.max(el, axis=-1, keepdims=True)
    i2 = jnp.min(jnp.where(el == e2, lanef, big), axis=-1, keepdims=True)
    t = jnp.exp(e2 - e1)
    w1 = g_w / (1.0 + t)
    w2 = g_w * t / (1.0 + t)
    ri = jnp.where(lane == 0, i1 - N_GROUPS,
                   jnp.where(lane == 1, i2 - N_GROUPS,
                             jnp.where(lane == 2, w1, jnp.where(lane == 3, w2, 0.0))))
    ri_ref[...] = ri


def _outproj(hm, ha, x, mod, g2, wo, wr, br):
    bsz, s, d = x.shape
    tm = min(TOKEN_TILE, s)
    dh = hm.shape[2]
    const = lambda b, i: (0, 0)
    tile = lambda w: pl.BlockSpec((None, tm, w), lambda b, i: (b, i, 0))
    modrow = lambda k: pl.BlockSpec((None, None, 1, d), lambda b, i: (b, k, 0, 0))
    return pl.pallas_call(
        _outproj_kernel,
        out_shape=(jax.ShapeDtypeStruct((bsz, s, d), F32),
                   jax.ShapeDtypeStruct((bsz, s, d), F32),
                   jax.ShapeDtypeStruct((bsz, s, LANES), F32)),
        grid=(bsz, s // tm),
        in_specs=[tile(dh), tile(dh), tile(d), modrow(2), modrow(3), modrow(4),
                  pl.BlockSpec((1, d), const),
                  pl.BlockSpec((2 * dh, d), const),
                  pl.BlockSpec((d, LANES), const),
                  pl.BlockSpec((1, LANES), const)],
        out_specs=(tile(d), tile(d), tile(LANES)),
        compiler_params=pltpu.CompilerParams(dimension_semantics=("arbitrary", "arbitrary"),
                                             vmem_limit_bytes=VMEM_LIMIT),
        name="outproj",
    )(hm, ha, x, mod, mod, mod, g2, wo, wr, br)


def _rank_kernel(ri_ref, dest_ref, cnt_ref, cnt_sc, run_sc):
    ph = pl.program_id(0)
    i = pl.program_id(1)
    tm = ri_ref.shape[0]
    lane = _iota((tm, LANES), 1).astype(F32)
    ri = ri_ref[...]
    oh0 = lane == ri[:, 0:1]
    oh1 = lane == ri[:, 1:2]
    oh = jnp.where(jnp.logical_or(oh0, oh1), 1.0, 0.0)
    colsum = jnp.sum(oh, axis=0, keepdims=True)

    @pl.when(jnp.logical_and(ph == 0, i == 0))
    def _():
        cnt_sc[...] = jnp.zeros_like(cnt_sc)

    @pl.when(ph == 0)
    def _():
        cnt_sc[...] = cnt_sc[...] + colsum
        dest_ref[...] = jnp.zeros_like(dest_ref)
        cnt_ref[...] = cnt_sc[...]

    @pl.when(jnp.logical_and(ph == 1, i == 0))
    def _():
        padded = jnp.ceil(cnt_sc[...] / ROW_BLOCK) * ROW_BLOCK
        l8 = _iota((SUBLANES, LANES), 1)
        acc = padded
        k = 1
        while k < LANES:
            acc = acc + jnp.where(l8 >= k, pltpu.roll(acc, k, 1), 0.0)
            k *= 2
        run_sc[...] = acc - padded

    @pl.when(ph == 1)
    def _():
        row = _iota((tm, tm), 0)
        col = _iota((tm, tm), 1)
        before = _dot((col < row).astype(BF16), oh.astype(BF16))
        tot = before + run_sc[0:1, :]
        d0 = jnp.sum(jnp.where(oh0, tot, 0.0), axis=-1, keepdims=True)
        d1 = jnp.sum(jnp.where(oh1, tot, 0.0), axis=-1, keepdims=True)
        lane_i = _iota((tm, LANES), 1)
        dest_ref[...] = jnp.where(lane_i == 0, d0, jnp.where(lane_i == 1, d1, 0.0))
        run_sc[...] = run_sc[...] + colsum
        cnt_ref[...] = cnt_sc[...]


def _rank(ri):
    t = ri.shape[0]
    tm = min(TOKEN_TILE, t)
    return pl.pallas_call(
        _rank_kernel,
        out_shape=(jax.ShapeDtypeStruct((t, LANES), F32),
                   jax.ShapeDtypeStruct((SUBLANES, LANES), F32)),
        grid=(2, t // tm),
        in_specs=[pl.BlockSpec((tm, LANES), lambda p, i: (i, 0))],
        out_specs=(pl.BlockSpec((tm, LANES), lambda p, i: (i * p, 0)),
                   pl.BlockSpec((SUBLANES, LANES), lambda p, i: (0, 0))),
        scratch_shapes=[pltpu.VMEM((SUBLANES, LANES), F32), pltpu.VMEM((SUBLANES, LANES), F32)],
        compiler_params=pltpu.CompilerParams(dimension_semantics=("arbitrary", "arbitrary")),
        name="rank",
    )(ri)


def _dispatch_kernel(dest_ref, hn_ref, xpad_in_ref, xpad_ref, sem):
    del xpad_in_ref
    tm = hn_ref.shape[0]

    def row_copy(t, k):
        d = dest_ref[0, 2 * t + k]
        return pltpu.make_async_copy(hn_ref.at[pl.ds(t, 1), :], xpad_ref.at[pl.ds(d, 1), :], sem)

    def issue(t, carry):
        row_copy(t, 0).start()
        row_copy(t, 1).start()
        return carry
    lax.fori_loop(0, tm, issue, 0, unroll=ROW_DMA_UNROLL)

    for _ in range(2):
        pltpu.make_async_copy(hn_ref, xpad_ref.at[pl.ds(0, tm), :], sem).wait()


def _dispatch(dest, hn, rows):
    t, d = hn.shape
    tm = min(TOKEN_TILE, t)
    xpad0 = jnp.zeros((rows, d), hn.dtype)
    return pl.pallas_call(
        _dispatch_kernel,
        out_shape=jax.ShapeDtypeStruct((rows, d), hn.dtype),
        grid=(t // tm,),
        in_specs=[pl.BlockSpec((None, 1, 2 * tm), lambda i: (i, 0, 0), memory_space=pltpu.SMEM),
                  pl.BlockSpec((tm, d), lambda i: (i, 0)),
                  pl.BlockSpec(memory_space=pl.ANY)],
        out_specs=pl.BlockSpec(memory_space=pl.ANY),
        scratch_shapes=[pltpu.SemaphoreType.DMA],
        input_output_aliases={2: 0},
        compiler_params=pltpu.CompilerParams(dimension_semantics=("arbitrary",)),
        name="dispatch",
    )(dest.reshape(t // tm, 1, 2 * tm), hn, xpad0)


def _expert_kernel(be_ref, nu_ref, x_ref, wg_ref, wu_ref, wd_ref, y_ref):
    i = pl.program_id(0)

    @pl.when(i < nu_ref[0])
    def _():
        xb = x_ref[...].astype(BF16)
        g = _dot(xb, wg_ref[...])
        u = _dot(xb, wu_ref[...])
        y_ref[...] = _dot((g * _sigmoid(g) * u).astype(BF16), wd_ref[...])

    @pl.when(i >= nu_ref[0])
    def _():
        y_ref[...] = jnp.zeros_like(y_ref)


def _experts(blk_expert, n_used, xpad, wg, wu, wd):
    rows, d = xpad.shape
    de = wg.shape[2]
    nblk = rows // ROW_BLOCK
    live = lambda i, be, nu: jnp.minimum(i, nu[0] - 1)
    return pl.pallas_call(
        _expert_kernel,
        out_shape=jax.ShapeDtypeStruct((rows, d), F32),
        grid_spec=pltpu.PrefetchScalarGridSpec(
            num_scalar_prefetch=2,
            grid=(nblk,),
            in_specs=[pl.BlockSpec((ROW_BLOCK, d), lambda i, be, nu: (live(i, be, nu), 0)),
                      pl.BlockSpec((None, d, de), lambda i, be, nu: (be[live(i, be, nu)], 0, 0)),
                      pl.BlockSpec((None, d, de), lambda i, be, nu: (be[live(i, be, nu)], 0, 0)),
                      pl.BlockSpec((None, de, d), lambda i, be, nu: (be[live(i, be, nu)], 0, 0))],
            out_specs=pl.BlockSpec((ROW_BLOCK, d), lambda i, be, nu: (i, 0))),
        compiler_params=pltpu.CompilerParams(dimension_semantics=("arbitrary",),
                                             vmem_limit_bytes=VMEM_LIMIT),
        name="experts",
    )(blk_expert, n_used, xpad, wg, wu, wd)


def _combine_kernel(final_norm, dest_ref, ypad_ref, x1_ref, ri_ref, gate_ref, gf_ref, out_ref, ybuf, sem):
    tm = x1_ref.shape[0]

    def row_copy(t, k):
        d = dest_ref[0, 2 * t + k]
        return pltpu.make_async_copy(ypad_ref.at[pl.ds(d, 1), :], ybuf.at[k, pl.ds(t, 1), :], sem)

    def issue(t, carry):
        row_copy(t, 0).start()
        row_copy(t, 1).start()
        return carry
    lax.fori_loop(0, tm, issue, 0, unroll=ROW_DMA_UNROLL)

    for k in range(2):
        pltpu.make_async_copy(ypad_ref.at[pl.ds(0, tm), :], ybuf.at[k], sem).wait()

    ri = ri_ref[...]
    ffn = ri[:, 2:3] * ybuf[0] + ri[:, 3:4] * ybuf[1]
    x2 = x1_ref[...] + gate_ref[...] * ffn
    if final_norm:
        ms = jnp.mean(x2 * x2, axis=-1, keepdims=True)
        x2 = x2 * lax.rsqrt(ms + EPS) * gf_ref[...]
    out_ref[...] = x2


def _combine(dest, ypad, x1, ri, mod, gf, final_norm):
    bsz, s, d = x1.shape
    tm = min(COMBINE_TILE, s)
    nt = s // tm
    return pl.pallas_call(
        functools.partial(_combine_kernel, final_norm),
        out_shape=jax.ShapeDtypeStruct((bsz, s, d), F32),
        grid=(bsz, nt),
        in_specs=[pl.BlockSpec((None, 1, 2 * tm), lambda b, i: (b * nt + i, 0, 0), memory_space=pltpu.SMEM),
                  pl.BlockSpec(memory_space=pl.ANY),
                  pl.BlockSpec((None, tm, d), lambda b, i: (b, i, 0)),
                  pl.BlockSpec((None, tm, LANES), lambda b, i: (b, i, 0)),
                  pl.BlockSpec((None, None, 1, d), lambda b, i: (b, 5, 0, 0)),
                  pl.BlockSpec((1, d), lambda b, i: (0, 0))],
        out_specs=pl.BlockSpec((None, tm, d), lambda b, i: (b, i, 0)),
        scratch_shapes=[pltpu.VMEM((2, tm, d), F32), pltpu.SemaphoreType.DMA],
        compiler_params=pltpu.CompilerParams(dimension_semantics=("arbitrary", "arbitrary"),
                                             vmem_limit_bytes=VMEM_LIMIT),
        name="combine",
    )(dest.reshape(bsz * nt, 1, 2 * tm), ypad, x1, ri.reshape(bsz, s, LANES), mod, gf)


def _pad_lanes(a):
    return jnp.pad(a, ((0, 0), (0, LANES - a.shape[1])))


def kernel(x, c, w_ada, b_ada, g_norm1, w_in, w_conv, b_conv, b_gates, g_mlstm_head, w_out, g_norm2,
           w_router_group, b_router_group, w_router_expert, b_router_expert, w_expert_gate, w_expert_up,
           w_expert_down, g_final):
    bsz, s, d = x.shape
    t = bsz * s
    dm = MLSTM_HEADS * MLSTM_HEAD_DIM
    da = MOBA_HEADS * MOBA_HEAD_DIM
    ng = 2 * MLSTM_HEADS
    n_assign = 2 * t
    n_blocks = -(-n_assign // ROW_BLOCK) + N_EXPERTS
    rows = n_blocks * ROW_BLOCK

    for l in range(w_ada.shape[0]):
        mod = _adaln(c, w_ada[l], b_ada[l]).reshape(bsz, 6, 1, d)

        w = w_in[l]
        wm = w[:, :4 * dm].astype(BF16)
        wgates = w[:, 4 * dm:4 * dm + ng]
        wa = w[:, 4 * dm + ng:4 * dm + ng + 2 * da].astype(BF16)
        wvt = w[:, 4 * dm + ng + 2 * da:].T.astype(BF16)
        pm, pa, vt, gc, gt = _inproj(x, mod, g_norm1[l].reshape(1, d), wm, wa, wvt,
                                     _pad_lanes(wgates).astype(BF16), wgates.T.astype(BF16),
                                     _pad_lanes(b_gates[l].reshape(1, ng)), b_gates[l].reshape(ng, 1))
        hm = _mlstm(pm, gc, gt, w_conv[l], b_conv[l], g_mlstm_head[l])
        ha = _moba(pa, vt)

        wr = _pad_lanes(jnp.concatenate([w_router_group[l], w_router_expert[l]], axis=1)).astype(BF16)
        br = _pad_lanes(jnp.concatenate([b_router_group[l], b_router_expert[l]]).reshape(1, -1))
        x1, hn2, ri = _outproj(hm, ha, x, mod, g_norm2[l].reshape(1, d), w_out[l].astype(BF16), wr, br)
        ri = ri.reshape(t, LANES)
        destf, cnt = _rank(ri)
        dest = destf[:, :2].astype(jnp.int32).reshape(n_assign)
        counts = cnt[0, :N_EXPERTS].astype(jnp.int32)
        pad_ends = jnp.cumsum((counts + ROW_BLOCK - 1) // ROW_BLOCK)
        blk_ids = jnp.arange(n_blocks, dtype=jnp.int32)
        blk_expert = jnp.minimum(jnp.sum((pad_ends[None, :] <= blk_ids[:, None]).astype(jnp.int32), axis=1),
                                 N_EXPERTS - 1)
        n_used = pad_ends[-1:].astype(jnp.int32)
        xpad = _dispatch(dest, hn2.reshape(t, d), rows)
        ypad = _experts(blk_expert, n_used, xpad, w_expert_gate[l].astype(BF16),
                        w_expert_up[l].astype(BF16), w_expert_down[l].astype(BF16))
        x = _combine(dest, ypad, x1, ri, mod, g_final.reshape(1, d), l == w_ada.shape[0] - 1)
    return x
```

```python
import functools

import jax
import jax.numpy as jnp
from jax import lax
from jax.experimental import pallas as pl
from jax.experimental.pallas import tpu as pltpu

F32 = jnp.float32
BF16 = jnp.bfloat16

MLSTM_HEADS = 4
MLSTM_HEAD_DIM = 128
CONV_WIDTH = 4
MOBA_HEADS = 8
MOBA_HEAD_DIM = 64
MOBA_BLOCK = 256
MOBA_TOPK = 3
N_GROUPS = 4
EXPERTS_PER_GROUP = 8
N_EXPERTS = N_GROUPS * EXPERTS_PER_GROUP
EPS = 1e-6

LANES = 128
SUBLANES = 8
ROW_BLOCK = 256
MLSTM_CHUNK = 256
TOKEN_TILE = 512
COMBINE_TILE = 256
ROW_DMA_UNROLL = 8
NEG = -1e30
LOG2E = 1.4426950408889634
MOBA_GROUP = 2
MOBA_GROUP_KEYS = MOBA_GROUP * MOBA_BLOCK
MOBA_STEP_GROUPS = 2
VMEM_LIMIT = 48 * 1024 * 1024


def _dot(a, b):
    return jnp.dot(a, b, preferred_element_type=F32)


def _dot_nt(a, b):
    return lax.dot_general(a, b, (((1,), (1,)), ((), ())), preferred_element_type=F32)


def _dot_tn(a, b):
    return lax.dot_general(a, b, (((0,), (0,)), ((), ())), preferred_element_type=F32)


def _sigmoid(x):
    return 1.0 / (1.0 + jnp.exp(-x))


def _log_sigmoid(x):
    return jnp.minimum(x, 0.0) - jnp.log1p(jnp.exp(-jnp.abs(x)))


def _split3(x):
    hi = x.astype(BF16)
    r1 = x - hi.astype(F32)
    mid = r1.astype(BF16)
    lo = (r1 - mid.astype(F32)).astype(BF16)
    return hi, mid, lo


def _iota(shape, dim):
    return lax.broadcasted_iota(jnp.int32, shape, dim)


def _load_rows(ref):
    n = ref.shape[0] // SUBLANES
    return jnp.concatenate([ref[pl.ds(j, n, stride=SUBLANES), :] for j in range(SUBLANES)], axis=1)


def _store_rows(ref, val):
    n = ref.shape[0] // SUBLANES
    for j in range(SUBLANES):
        ref[pl.ds(j, n, stride=SUBLANES), :] = val[:, j * LANES:(j + 1) * LANES]


def _token_tile(ref, t):
    return ref.at[pl.ds(pl.multiple_of(t * SUBLANES, SUBLANES), SUBLANES), :]


def _adaln_kernel(c_ref, w_ref, b_ref, o_ref):
    c = c_ref[...]
    o_ref[...] = _dot(c * _sigmoid(c), w_ref[...]) + b_ref[...]


def _adaln(c, w, b):
    bsz, d = c.shape
    n = w.shape[1]
    tn = 1024
    return pl.pallas_call(
        _adaln_kernel,
        out_shape=jax.ShapeDtypeStruct((bsz, n), F32),
        grid=(n // tn,),
        in_specs=[pl.BlockSpec((bsz, d), lambda j: (0, 0)),
                  pl.BlockSpec((d, tn), lambda j: (0, j)),
                  pl.BlockSpec((1, tn), lambda j: (0, j))],
        out_specs=pl.BlockSpec((bsz, tn), lambda j: (0, j)),
        name="adaln",
    )(c, w, b.reshape(1, n))


def _inproj_kernel(x_ref, shift_ref, scale_ref, g_ref, wm_ref, wa_ref, wvt_ref, wg_ref, wgt_ref, bg_ref, bgt_ref,
                   pm_ref, pa_ref, vt_ref, gc_ref, gt_ref):
    x = x_ref[...]
    ms = jnp.mean(x * x, axis=-1, keepdims=True)
    y = x * lax.rsqrt(ms + EPS) * g_ref[...]
    hb = (y * (1.0 + scale_ref[...]) + shift_ref[...]).astype(BF16)
    cw = 512
    for j in range(wm_ref.shape[1] // cw):
        pm_ref[:, j * cw:(j + 1) * cw] = _dot(hb, wm_ref[:, j * cw:(j + 1) * cw])
    for j in range(wa_ref.shape[1] // cw):
        pa_ref[:, j * cw:(j + 1) * cw] = _dot(hb, wa_ref[:, j * cw:(j + 1) * cw]).astype(BF16)
    vt = _dot_nt(wvt_ref[...], hb).astype(BF16)
    for j in range(vt_ref.shape[0]):
        vt_ref[j] = vt[:, j * MOBA_GROUP_KEYS:(j + 1) * MOBA_GROUP_KEYS]
    gc_ref[...] = _dot(hb, wg_ref[...]) + bg_ref[...]
    gt_ref[...] = _dot_nt(wgt_ref[...], hb) + bgt_ref[...]


def _inproj(x, mod, g1, wm, wa, wvt, wg, wgt, bg, bgt):
    bsz, s, d = x.shape
    tm = min(TOKEN_TILE, s)
    nm, na, nv = wm.shape[1], wa.shape[1], wvt.shape[0]
    vb = tm // MOBA_GROUP_KEYS
    ng = wgt.shape[0]
    const = lambda b, i: (0, 0)
    return pl.pallas_call(
        _inproj_kernel,
        out_shape=(jax.ShapeDtypeStruct((bsz, s, nm), F32),
                   jax.ShapeDtypeStruct((bsz, s, na), BF16),
                   jax.ShapeDtypeStruct((bsz, s // MOBA_GROUP_KEYS, nv, MOBA_GROUP_KEYS), BF16),
                   jax.ShapeDtypeStruct((bsz, s, LANES), F32),
                   jax.ShapeDtypeStruct((bsz, ng, s), F32)),
        grid=(bsz, s // tm),
        in_specs=[pl.BlockSpec((None, tm, d), lambda b, i: (b, i, 0)),
                  pl.BlockSpec((None, None, 1, d), lambda b, i: (b, 0, 0, 0)),
                  pl.BlockSpec((None, None, 1, d), lambda b, i: (b, 1, 0, 0)),
                  pl.BlockSpec((1, d), const),
                  pl.BlockSpec((d, nm), const),
                  pl.BlockSpec((d, na), const),
                  pl.BlockSpec((nv, d), const),
                  pl.BlockSpec((d, LANES), const),
                  pl.BlockSpec((ng, d), const),
                  pl.BlockSpec((1, LANES), const),
                  pl.BlockSpec((ng, 1), const)],
        out_specs=(pl.BlockSpec((None, tm, nm), lambda b, i: (b, i, 0)),
                   pl.BlockSpec((None, tm, na), lambda b, i: (b, i, 0)),
                   pl.BlockSpec((None, vb, nv, MOBA_GROUP_KEYS), lambda b, i: (b, i, 0, 0)),
                   pl.BlockSpec((None, tm, LANES), lambda b, i: (b, i, 0)),
                   pl.BlockSpec((None, ng, tm), lambda b, i: (b, 0, i))),
        compiler_params=pltpu.CompilerParams(dimension_semantics=("arbitrary", "arbitrary"),
                                             vmem_limit_bytes=VMEM_LIMIT),
        name="inproj",
    )(x, mod, mod, g1, wm, wa, wvt, wg, wgt, bg, bgt)


def _mlstm_kernel(q_ref, k_ref, v_ref, o_ref, gc_ref, gt_ref, wc_ref, bc_ref, gh_ref, out_ref,
                  qbuf, kbuf, c_sc, n_sc, m_sc):
    L = q_ref.shape[0]
    H, DH = MLSTM_HEADS, MLSTM_HEAD_DIM
    HD = H * DH
    ci = pl.program_id(1)

    @pl.when(ci == 0)
    def _():
        qbuf[0:SUBLANES, :] = jnp.zeros((SUBLANES, HD), F32)
        kbuf[0:SUBLANES, :] = jnp.zeros((SUBLANES, HD), F32)
        c_sc[...] = jnp.zeros_like(c_sc)
        n_sc[...] = jnp.zeros_like(n_sc)
        m_sc[...] = jnp.zeros_like(m_sc)

    qbuf[SUBLANES:SUBLANES + L, :] = q_ref[...]
    kbuf[SUBLANES:SUBLANES + L, :] = k_ref[...]
    qc = jnp.broadcast_to(bc_ref[:, 0:HD], (L, HD))
    kc = jnp.broadcast_to(bc_ref[:, HD:2 * HD], (L, HD))
    for j in range(CONV_WIDTH):
        off = SUBLANES - (CONV_WIDTH - 1) + j
        qc = qc + wc_ref[j:j + 1, 0:HD] * qbuf[off:off + L, :]
        kc = kc + wc_ref[j:j + 1, HD:2 * HD] * kbuf[off:off + L, :]
    qbuf[0:SUBLANES, :] = qbuf[L:L + SUBLANES, :]
    kbuf[0:SUBLANES, :] = kbuf[L:L + SUBLANES, :]
    qa = qc * _sigmoid(qc) * (DH ** -0.5)
    ka = kc * _sigmoid(kc)

    gc = gc_ref[...]
    gt = gt_ref[...]
    row = _iota((L, L), 0)
    col = _iota((L, L), 1)
    causal = col <= row
    tril = causal.astype(BF16)
    triu = (row <= col).astype(BF16)
    h3 = _split3(_log_sigmoid(gc))
    cum_c = _dot(tril, h3[0]) + _dot(tril, h3[1]) + _dot(tril, h3[2])
    r3 = _split3(_log_sigmoid(gt))
    cum_r = _dot(r3[0], triu) + _dot(r3[1], triu) + _dot(r3[2], triu)

    for h in range(H):
        sl = slice(h * DH, (h + 1) * DH)
        qf = qa[:, sl]
        qb = qf.astype(BF16)
        kf = ka[:, sl]
        kb = kf.astype(BF16)
        vb = v_ref[:, sl].astype(BF16)
        a_c = cum_c[:, H + h:H + h + 1]
        li_c = gc[:, h:h + 1]
        b_r = gt[h:h + 1, :] - cum_r[H + h:H + h + 1, :]
        cum_last = cum_r[H + h:H + h + 1, L - 1:L]
        m_prev = m_sc[h:h + 1, 0:1]
        c_prev = c_sc[h]
        n_prev = n_sc[h:h + 1, :]

        dm = jnp.where(causal, a_c + b_r, -jnp.inf)
        inter = a_c + m_prev
        m_t = jnp.maximum(inter, jnp.max(dm, axis=-1, keepdims=True))
        w_inter = jnp.exp(inter - m_t)
        sc = _dot_nt(qb, kb) * jnp.exp(dm - m_t)
        num = w_inter * _dot(qb, c_prev.astype(BF16)) + _dot(sc.astype(BF16), vb)
        den = w_inter * jnp.sum(qf * n_prev, axis=-1, keepdims=True) + jnp.sum(sc, axis=-1, keepdims=True)
        ht = num / jnp.maximum(jnp.abs(den), jnp.exp(-m_t))

        g_end_r = cum_last + b_r
        m_new = jnp.maximum(cum_last + m_prev, jnp.max(g_end_r, axis=-1, keepdims=True))
        w_old = jnp.exp(cum_last + m_prev - m_new)
        w_new_c = jnp.exp(cum_last + li_c - a_c - m_new)
        kw = kf * w_new_c
        c_sc[h] = w_old * c_prev + _dot_tn(kw.astype(BF16), vb)
        n_sc[h:h + 1, :] = w_old * n_prev + jnp.sum(kw, axis=0, keepdims=True)
        m_sc[h:h + 1, :] = jnp.broadcast_to(m_new, (1, LANES))

        hn = ht * lax.rsqrt(jnp.mean(ht * ht, axis=-1, keepdims=True) + EPS) * gh_ref[:, sl]
        out_ref[:, sl] = (hn * _sigmoid(o_ref[:, sl])).astype(BF16)


def _mlstm(pm, gc, gt, w_conv, b_conv, g_head):
    bsz, s, _ = pm.shape
    L = min(MLSTM_CHUNK, s)
    H, DH = MLSTM_HEADS, MLSTM_HEAD_DIM
    HD = H * DH
    const = lambda b, i: (0, 0)
    col_block = lambda j: pl.BlockSpec((None, L, HD), lambda b, i: (b, i, j))
    return pl.pallas_call(
        _mlstm_kernel,
        out_shape=jax.ShapeDtypeStruct((bsz, s, HD), BF16),
        grid=(bsz, s // L),
        in_specs=[col_block(0), col_block(1), col_block(2), col_block(3),
                  pl.BlockSpec((None, L, LANES), lambda b, i: (b, i, 0)),
                  pl.BlockSpec((None, 2 * H, L), lambda b, i: (b, 0, i)),
                  pl.BlockSpec((CONV_WIDTH, 2 * HD), const),
                  pl.BlockSpec((1, 2 * HD), const),
                  pl.BlockSpec((1, HD), const)],
        out_specs=pl.BlockSpec((None, L, HD), lambda b, i: (b, i, 0)),
        scratch_shapes=[pltpu.VMEM((L + SUBLANES, HD), F32),
                        pltpu.VMEM((L + SUBLANES, HD), F32),
                        pltpu.VMEM((H, DH, DH), F32),
                        pltpu.VMEM((SUBLANES, DH), F32),
                        pltpu.VMEM((SUBLANES, LANES), F32)],
        compiler_params=pltpu.CompilerParams(dimension_semantics=("arbitrary", "arbitrary"),
                                             vmem_limit_bytes=VMEM_LIMIT),
        name="mlstm",
    )(pm, pm, pm, pm, gc, gt, w_conv, b_conv.reshape(1, 2 * HD), g_head.reshape(1, HD))


def _moba_kernel(q_ref, k_ref, vt_ref, out_ref, km_sc, bias_sc, sa_sc, sb_sc):
    BS = MOBA_BLOCK
    DHA = MOBA_HEAD_DIM
    GB = MOBA_GROUP
    GK = GB * BS
    nb = k_ref.shape[0] // BS
    ngroups = nb // GB
    qi = pl.program_id(2)

    @pl.when(qi == 0)
    def _():
        def mean_body(j, carry):
            kj = k_ref[pl.ds(pl.multiple_of(j * BS, BS), BS), :].astype(F32)
            km_sc[pl.ds(j, 1), :] = jnp.mean(kj, axis=0, keepdims=True)
            return carry
        lax.fori_loop(0, nb, mean_body, 0)

    lane = _iota((BS, LANES), 1)
    first = lane < DHA
    q = q_ref[...].astype(F32) * (DHA ** -0.5 * LOG2E)
    qh = (jnp.where(first, q, 0.0).astype(BF16), jnp.where(first, 0.0, q).astype(BF16))

    ones = jnp.ones((2 * SUBLANES, GK), BF16)

    def values(g, h):
        return jnp.concatenate([vt_ref[g, h * DHA:(h + 1) * DHA, :], ones], axis=0)

    def bias_row(h, g, c):
        return bias_sc[h, pl.ds(GB * g + c, 1), :]

    def scores(g, h, dst):
        gk = jnp.minimum(g, ngroups - 1)
        mx = []
        for c in range(GB):
            rows = pl.ds(pl.multiple_of(gk * GK + c * BS, BS), BS)
            s = _dot_nt(k_ref[rows, :], qh[h])
            dst[h, c] = s
            mx.append(jnp.max(s, axis=0, keepdims=True))
        return mx

    def weights(g, h, src, mx, state):
        m_old, acc = state
        bias = [bias_row(h, g, c) for c in range(GB)]
        mn = m_old
        for c in range(GB):
            mn = jnp.maximum(mn, mx[c] + bias[c])
        p = [jnp.exp2(src[h, c] - (mn - bias[c])).astype(BF16) for c in range(GB)]
        pv = _dot(values(jnp.minimum(g, ngroups - 1), h), jnp.concatenate(p, axis=0))
        return mn, acc * jnp.exp2(m_old - mn) + pv

    kmb = km_sc[...].astype(BF16)
    gate = [_dot_nt(kmb, qh[h]) for h in range(2)]
    own = pl.ds(pl.multiple_of(qi * BS, BS), BS)
    s_own = [_dot_nt(k_ref[own, :], qh[h]) for h in range(2)]
    mx_a = [scores(0, h, sa_sc) for h in range(2)]

    blk = _iota((nb, BS), 0)
    blkf = blk.astype(F32)
    for h in range(2):
        g = jnp.where(blk < qi, gate[h], -jnp.inf)
        picked = jnp.zeros((nb, BS), jnp.bool_)
        for _ in range(min(MOBA_TOPK, nb - 1)):
            mx = jnp.max(g, axis=0, keepdims=True)
            idx = jnp.min(jnp.where(g == mx, blkf, float(nb)), axis=0, keepdims=True)
            pick = blkf == idx
            picked = jnp.logical_or(picked, pick)
            g = jnp.where(pick, -jnp.inf, g)
        bias_sc[h, 0:nb] = jnp.where(jnp.logical_and(picked, blk < qi), 0.0, NEG)
        bias_sc[h, nb:nb + SUBLANES] = jnp.full((SUBLANES, BS), NEG, F32)

    key_pos = _iota((BS, BS), 0)
    query_pos = _iota((BS, BS), 1)
    causal = key_pos <= query_pos
    own_cols = pl.ds(pl.multiple_of((qi % GB) * BS, BS), BS)
    state = []
    for h in range(2):
        s = jnp.where(causal, s_own[h], NEG)
        mn = jnp.max(s, axis=0, keepdims=True)
        p = jnp.exp2(s - mn).astype(BF16)
        v_own = jnp.concatenate([vt_ref[qi // GB, h * DHA:(h + 1) * DHA, own_cols], ones[:, 0:BS]], axis=0)
        state.append((mn, _dot(v_own, p)))

    NG = MOBA_STEP_GROUPS

    def past_step(t, carry):
        state, mx_a = carry
        for u in range(0, NG, 2):
            g = NG * t + u
            mx_b = [scores(g + 1, h, sb_sc) for h in range(2)]
            state = [weights(g, h, sa_sc, mx_a[h], state[h]) for h in range(2)]
            mx_a = [scores(g + 2, h, sa_sc) for h in range(2)]
            state = [weights(g + 1, h, sb_sc, mx_b[h], state[h]) for h in range(2)]
        return state, mx_a

    state, _ = lax.fori_loop(0, (qi + (NG * GB - 1)) // (NG * GB), past_step, (state, mx_a))
    out_t = jnp.concatenate([state[h][1][0:DHA] / state[h][1][DHA:DHA + 1] for h in range(2)], axis=0)
    out_ref[...] = out_t.T.astype(BF16)


def _moba(pa, vt):
    bsz, s, _ = pa.shape
    BS = MOBA_BLOCK
    npair = MOBA_HEADS * MOBA_HEAD_DIM // LANES
    nb = s // BS
    gb = MOBA_GROUP
    assert nb % gb == 0 and gb <= SUBLANES
    return pl.pallas_call(
        _moba_kernel,
        out_shape=jax.ShapeDtypeStruct((bsz, s, npair * LANES), BF16),
        grid=(bsz, npair, nb),
        in_specs=[pl.BlockSpec((None, BS, LANES), lambda b, p, i: (b, i, p)),
                  pl.BlockSpec((None, s, LANES), lambda b, p, i: (b, 0, npair + p)),
                  pl.BlockSpec((None, nb // gb, LANES, gb * BS), lambda b, p, i: (b, 0, p, 0))],
        out_specs=pl.BlockSpec((None, BS, LANES), lambda b, p, i: (b, i, p)),
        scratch_shapes=[pltpu.VMEM((nb, LANES), F32), pltpu.VMEM((2, nb + SUBLANES, BS), F32),
                        pltpu.VMEM((2, gb, BS, BS), F32), pltpu.VMEM((2, gb, BS, BS), F32)],
        compiler_params=pltpu.CompilerParams(dimension_semantics=("arbitrary",) * 3,
                                             vmem_limit_bytes=VMEM_LIMIT),
        name="moba",
    )(pa, pa, vt)


def _outproj_kernel(hm_ref, ha_ref, x_ref, gate_ref, shift_ref, scale_ref, g_ref, wo_ref, wr_ref, br_ref,
                    x1_ref, hn_ref, ri_ref):
    dm = hm_ref.shape[1]
    mix = _dot(hm_ref[...], wo_ref[0:dm, :]) + _dot(ha_ref[...], wo_ref[dm:, :])
    x1 = x_ref[...] + gate_ref[...] * mix
    x1_ref[...] = x1
    ms = jnp.mean(x1 * x1, axis=-1, keepdims=True)
    hn = x1 * lax.rsqrt(ms + EPS) * g_ref[...] * (1.0 + scale_ref[...]) + shift_ref[...]
    _store_rows(hn_ref, hn)
    logits = _dot(hn.astype(BF16), wr_ref[...]) + br_ref[...]

    tm = logits.shape[0]
    lane = _iota((tm, LANES), 1)
    lanef = lane.astype(F32)
    big = float(LANES)
    in_groups = lane < N_GROUPS
    gl = jnp.where(in_groups, logits, -jnp.inf)
    gmax = jnp.max(gl, axis=-1, keepdims=True)
    gidx = jnp.min(jnp.where(gl == gmax, lanef, big), axis=-1, keepdims=True)
    g_w = 1.0 / jnp.sum(jnp.where(in_groups, jnp.exp(gl - gmax), 0.0), axis=-1, keepdims=True)
    lo = N_GROUPS + EXPERTS_PER_GROUP * gidx
    el = jnp.where(jnp.logical_and(lanef >= lo, lanef < lo + EXPERTS_PER_GROUP), logits, -jnp.inf)
    e1 = jnp.max(el, axis=-1, keepdims=True)
    i1 = jnp.min(jnp.where(el == e1, lanef, big), axis=-1, keepdims=True)
    el = jnp.where(lanef == i1, -jnp.inf, el)
    e2 = jnp.max(el, axis=-1, keepdims=True)
    i2 = jnp.min(jnp.where(el == e2, lanef, big), axis=-1, keepdims=True)
    t = jnp.exp(e2 - e1)
    w1 = g_w / (1.0 + t)
    w2 = g_w * t / (1.0 + t)
    ri = jnp.where(lane == 0, i1 - N_GROUPS,
                   jnp.where(lane == 1, i2 - N_GROUPS,
                             jnp.where(lane == 2, w1, jnp.where(lane == 3, w2, 0.0))))
    ri_ref[...] = ri


def _outproj(hm, ha, x, mod, g2, wo, wr, br):
    bsz, s, d = x.shape
    tm = min(TOKEN_TILE, s)
    dh = hm.shape[2]
    const = lambda b, i: (0, 0)
    tile = lambda w: pl.BlockSpec((None, tm, w), lambda b, i: (b, i, 0))
    modrow = lambda k: pl.BlockSpec((None, None, 1, d), lambda b, i: (b, k, 0, 0))
    return pl.pallas_call(
        _outproj_kernel,
        out_shape=(jax.ShapeDtypeStruct((bsz, s, d), F32),
                   jax.ShapeDtypeStruct((bsz, s * SUBLANES, LANES), F32),
                   jax.ShapeDtypeStruct((bsz, s, LANES), F32)),
        grid=(bsz, s // tm),
        in_specs=[tile(dh), tile(dh), tile(d), modrow(2), modrow(3), modrow(4),
                  pl.BlockSpec((1, d), const),
                  pl.BlockSpec((2 * dh, d), const),
                  pl.BlockSpec((d, LANES), const),
                  pl.BlockSpec((1, LANES), const)],
        out_specs=(tile(d), pl.BlockSpec((None, tm * SUBLANES, LANES), lambda b, i: (b, i, 0)), tile(LANES)),
        compiler_params=pltpu.CompilerParams(dimension_semantics=("arbitrary", "arbitrary"),
                                             vmem_limit_bytes=VMEM_LIMIT),
        name="outproj",
    )(hm, ha, x, mod, mod, mod, g2, wo, wr, br)


def _rank_kernel(ri_ref, dest_ref, cnt_ref, cnt_sc, run_sc):
    ph = pl.program_id(0)
    i = pl.program_id(1)
    tm = ri_ref.shape[0]
    lane = _iota((tm, LANES), 1).astype(F32)
    ri = ri_ref[...]
    oh0 = lane == ri[:, 0:1]
    oh1 = lane == ri[:, 1:2]
    oh = jnp.where(jnp.logical_or(oh0, oh1), 1.0, 0.0)
    colsum = jnp.sum(oh, axis=0, keepdims=True)

    @pl.when(jnp.logical_and(ph == 0, i == 0))
    def _():
        cnt_sc[...] = jnp.zeros_like(cnt_sc)

    @pl.when(ph == 0)
    def _():
        cnt_sc[...] = cnt_sc[...] + colsum
        dest_ref[...] = jnp.zeros_like(dest_ref)
        cnt_ref[...] = cnt_sc[...]

    @pl.when(jnp.logical_and(ph == 1, i == 0))
    def _():
        padded = jnp.ceil(cnt_sc[...] / ROW_BLOCK) * ROW_BLOCK
        l8 = _iota((SUBLANES, LANES), 1)
        acc = padded
        k = 1
        while k < LANES:
            acc = acc + jnp.where(l8 >= k, pltpu.roll(acc, k, 1), 0.0)
            k *= 2
        run_sc[...] = acc - padded

    @pl.when(ph == 1)
    def _():
        row = _iota((tm, tm), 0)
        col = _iota((tm, tm), 1)
        before = _dot((col < row).astype(BF16), oh.astype(BF16))
        tot = before + run_sc[0:1, :]
        d0 = jnp.sum(jnp.where(oh0, tot, 0.0), axis=-1, keepdims=True)
        d1 = jnp.sum(jnp.where(oh1, tot, 0.0), axis=-1, keepdims=True)
        lane_i = _iota((tm, LANES), 1)
        dest_ref[...] = jnp.where(lane_i == 0, d0, jnp.where(lane_i == 1, d1, 0.0))
        run_sc[...] = run_sc[...] + colsum
        cnt_ref[...] = cnt_sc[...]


def _rank(ri):
    t = ri.shape[0]
    tm = min(TOKEN_TILE, t)
    return pl.pallas_call(
        _rank_kernel,
        out_shape=(jax.ShapeDtypeStruct((t, LANES), F32),
                   jax.ShapeDtypeStruct((SUBLANES, LANES), F32)),
        grid=(2, t // tm),
        in_specs=[pl.BlockSpec((tm, LANES), lambda p, i: (i, 0))],
        out_specs=(pl.BlockSpec((tm, LANES), lambda p, i: (i * p, 0)),
                   pl.BlockSpec((SUBLANES, LANES), lambda p, i: (0, 0))),
        scratch_shapes=[pltpu.VMEM((SUBLANES, LANES), F32), pltpu.VMEM((SUBLANES, LANES), F32)],
        compiler_params=pltpu.CompilerParams(dimension_semantics=("arbitrary", "arbitrary")),
        name="rank",
    )(ri)


def _dispatch_kernel(pe_ref, nu_ref, dest_ref, hn_ref, xpad_ref, zbuf, sem, zsem):
    tm = hn_ref.shape[0] // SUBLANES
    block_rows = ROW_BLOCK * SUBLANES
    nblk = xpad_ref.shape[0] // block_rows

    def zero_block(j):
        rows = pl.ds(pl.multiple_of(j * block_rows, block_rows), block_rows)
        return pltpu.make_async_copy(zbuf, xpad_ref.at[rows, :], zsem)

    @pl.when(pl.program_id(0) == 0)
    def _():
        zbuf[...] = jnp.zeros_like(zbuf)

        def tail(e, n):
            first = jnp.where(e == 0, 0, pe_ref[jnp.maximum(e - 1, 0)])
            nonempty = pe_ref[e] > first

            @pl.when(nonempty)
            def _():
                zero_block(pe_ref[e] - 1).start()
            return n + nonempty.astype(jnp.int32)
        started = lax.fori_loop(0, N_EXPERTS, tail, 0)

        def unused(j, c):
            zero_block(j).start()
            return c
        lax.fori_loop(nu_ref[0], nblk, unused, 0)

        def drain(j, c):
            zero_block(0).wait()
            return c
        lax.fori_loop(0, started + nblk - nu_ref[0], drain, 0)

    def row_copy(t, k):
        d = dest_ref[0, 2 * t + k]
        return pltpu.make_async_copy(_token_tile(hn_ref, t), _token_tile(xpad_ref, d), sem)

    def issue(t, carry):
        row_copy(t, 0).start()
        row_copy(t, 1).start()
        return carry
    lax.fori_loop(0, tm, issue, 0, unroll=ROW_DMA_UNROLL)

    for _ in range(2):
        pltpu.make_async_copy(hn_ref, xpad_ref.at[pl.ds(0, tm * SUBLANES), :], sem).wait()


def _dispatch(pad_ends, n_used, dest, hn, rows):
    t = hn.shape[0] // SUBLANES
    tm = min(TOKEN_TILE, t)
    return pl.pallas_call(
        _dispatch_kernel,
        out_shape=jax.ShapeDtypeStruct((rows * SUBLANES, LANES), hn.dtype),
        grid_spec=pltpu.PrefetchScalarGridSpec(
            num_scalar_prefetch=2,
            grid=(t // tm,),
            in_specs=[pl.BlockSpec((None, 1, 2 * tm), lambda i, pe, nu: (i, 0, 0), memory_space=pltpu.SMEM),
                      pl.BlockSpec((tm * SUBLANES, LANES), lambda i, pe, nu: (i, 0))],
            out_specs=pl.BlockSpec(memory_space=pl.ANY),
            scratch_shapes=[pltpu.VMEM((ROW_BLOCK * SUBLANES, LANES), hn.dtype),
                            pltpu.SemaphoreType.DMA, pltpu.SemaphoreType.DMA]),
        compiler_params=pltpu.CompilerParams(dimension_semantics=("arbitrary",)),
        name="dispatch",
    )(pad_ends, n_used, dest.reshape(t // tm, 1, 2 * tm), hn)


def _expert_kernel(be_ref, nu_ref, x_ref, wg_ref, wu_ref, wd_ref, y_ref, wgb, wub, wdb):
    i = pl.program_id(0)
    live = i < nu_ref[0]
    new_expert = jnp.logical_or(i == 0, be_ref[i] != be_ref[jnp.maximum(i - 1, 0)])

    @pl.when(jnp.logical_and(live, new_expert))
    def _():
        wgb[...] = wg_ref[...].astype(BF16)
        wub[...] = wu_ref[...].astype(BF16)
        wdb[...] = wd_ref[...].astype(BF16)

    @pl.when(live)
    def _():
        xb = _load_rows(x_ref).astype(BF16)
        g = _dot(xb, wgb[...])
        u = _dot(xb, wub[...])
        _store_rows(y_ref, _dot((g * _sigmoid(g) * u).astype(BF16), wdb[...]))

    @pl.when(jnp.logical_not(live))
    def _():
        y_ref[...] = jnp.zeros_like(y_ref)


def _experts(blk_expert, n_used, xpad, wg, wu, wd):
    rows = xpad.shape[0] // SUBLANES
    d, de = wg.shape[1], wg.shape[2]
    assert d == SUBLANES * LANES
    row_block = (ROW_BLOCK * SUBLANES, LANES)
    nblk = rows // ROW_BLOCK
    live = lambda i, be, nu: jnp.minimum(i, nu[0] - 1)
    return pl.pallas_call(
        _expert_kernel,
        out_shape=jax.ShapeDtypeStruct(xpad.shape, F32),
        grid_spec=pltpu.PrefetchScalarGridSpec(
            num_scalar_prefetch=2,
            grid=(nblk,),
            in_specs=[pl.BlockSpec(row_block, lambda i, be, nu: (live(i, be, nu), 0)),
                      pl.BlockSpec((None, d, de), lambda i, be, nu: (be[live(i, be, nu)], 0, 0)),
                      pl.BlockSpec((None, d, de), lambda i, be, nu: (be[live(i, be, nu)], 0, 0)),
                      pl.BlockSpec((None, de, d), lambda i, be, nu: (be[live(i, be, nu)], 0, 0))],
            out_specs=pl.BlockSpec(row_block, lambda i, be, nu: (i, 0)),
            scratch_shapes=[pltpu.VMEM((d, de), BF16), pltpu.VMEM((d, de), BF16), pltpu.VMEM((de, d), BF16)]),
        compiler_params=pltpu.CompilerParams(dimension_semantics=("arbitrary",),
                                             vmem_limit_bytes=VMEM_LIMIT),
        name="experts",
    )(blk_expert, n_used, xpad, wg, wu, wd)


def _combine_kernel(final_norm, dest_ref, ypad_ref, x1_ref, ri_ref, gate_ref, gf_ref, out_ref, ybuf, sem):
    tm = x1_ref.shape[0]

    def row_copy(t, k):
        d = dest_ref[0, 2 * t + k]
        return pltpu.make_async_copy(_token_tile(ypad_ref, d), _token_tile(ybuf.at[k], t), sem)

    def issue(t, carry):
        row_copy(t, 0).start()
        row_copy(t, 1).start()
        return carry
    lax.fori_loop(0, tm, issue, 0, unroll=ROW_DMA_UNROLL)

    for k in range(2):
        pltpu.make_async_copy(ypad_ref.at[pl.ds(0, tm * SUBLANES), :], ybuf.at[k], sem).wait()

    ri = ri_ref[...]
    ffn = ri[:, 2:3] * _load_rows(ybuf.at[0]) + ri[:, 3:4] * _load_rows(ybuf.at[1])
    x2 = x1_ref[...] + gate_ref[...] * ffn
    if final_norm:
        ms = jnp.mean(x2 * x2, axis=-1, keepdims=True)
        x2 = x2 * lax.rsqrt(ms + EPS) * gf_ref[...]
    out_ref[...] = x2


def _combine(dest, ypad, x1, ri, mod, gf, final_norm):
    bsz, s, d = x1.shape
    tm = min(COMBINE_TILE, s)
    nt = s // tm
    return pl.pallas_call(
        functools.partial(_combine_kernel, final_norm),
        out_shape=jax.ShapeDtypeStruct((bsz, s, d), F32),
        grid=(bsz, nt),
        in_specs=[pl.BlockSpec((None, 1, 2 * tm), lambda b, i: (b * nt + i, 0, 0), memory_space=pltpu.SMEM),
                  pl.BlockSpec(memory_space=pl.ANY),
                  pl.BlockSpec((None, tm, d), lambda b, i: (b, i, 0)),
                  pl.BlockSpec((None, tm, LANES), lambda b, i: (b, i, 0)),
                  pl.BlockSpec((None, None, 1, d), lambda b, i: (b, 5, 0, 0)),
                  pl.BlockSpec((1, d), lambda b, i: (0, 0))],
        out_specs=pl.BlockSpec((None, tm, d), lambda b, i: (b, i, 0)),
        scratch_shapes=[pltpu.VMEM((2, tm * SUBLANES, LANES), F32), pltpu.SemaphoreType.DMA],
        compiler_params=pltpu.CompilerParams(dimension_semantics=("arbitrary", "arbitrary"),
                                             vmem_limit_bytes=VMEM_LIMIT),
        name="combine",
    )(dest.reshape(bsz * nt, 1, 2 * tm), ypad, x1, ri.reshape(bsz, s, LANES), mod, gf)


def _pad_lanes(a):
    return jnp.pad(a, ((0, 0), (0, LANES - a.shape[1])))


def kernel(x, c, w_ada, b_ada, g_norm1, w_in, w_conv, b_conv, b_gates, g_mlstm_head, w_out, g_norm2,
           w_router_group, b_router_group, w_router_expert, b_router_expert, w_expert_gate, w_expert_up,
           w_expert_down, g_final):
    bsz, s, d = x.shape
    t = bsz * s
    dm = MLSTM_HEADS * MLSTM_HEAD_DIM
    da = MOBA_HEADS * MOBA_HEAD_DIM
    ng = 2 * MLSTM_HEADS
    n_assign = 2 * t
    n_blocks = -(-n_assign // ROW_BLOCK) + N_EXPERTS
    rows = n_blocks * ROW_BLOCK

    for l in range(w_ada.shape[0]):
        mod = _adaln(c, w_ada[l], b_ada[l]).reshape(bsz, 6, 1, d)

        w = w_in[l]
        wm = w[:, :4 * dm].astype(BF16)
        wgates = w[:, 4 * dm:4 * dm + ng]
        wa = w[:, 4 * dm + ng:4 * dm + ng + 2 * da].astype(BF16)
        wvt = w[:, 4 * dm + ng + 2 * da:].T.astype(BF16)
        pm, pa, vt, gc, gt = _inproj(x, mod, g_norm1[l].reshape(1, d), wm, wa, wvt,
                                     _pad_lanes(wgates).astype(BF16), wgates.T.astype(BF16),
                                     _pad_lanes(b_gates[l].reshape(1, ng)), b_gates[l].reshape(ng, 1))
        hm = _mlstm(pm, gc, gt, w_conv[l], b_conv[l], g_mlstm_head[l])
        ha = _moba(pa, vt)

        wr = _pad_lanes(jnp.concatenate([w_router_group[l], w_router_expert[l]], axis=1)).astype(BF16)
        br = _pad_lanes(jnp.concatenate([b_router_group[l], b_router_expert[l]]).reshape(1, -1))
        x1, hn2, ri = _outproj(hm, ha, x, mod, g_norm2[l].reshape(1, d), w_out[l].astype(BF16), wr, br)
        ri = ri.reshape(t, LANES)
        destf, cnt = _rank(ri)
        dest = destf[:, :2].astype(jnp.int32).reshape(n_assign)
        counts = cnt[0, :N_EXPERTS].astype(jnp.int32)
        pad_ends = jnp.cumsum((counts + ROW_BLOCK - 1) // ROW_BLOCK)
        blk_ids = jnp.arange(n_blocks, dtype=jnp.int32)
        blk_expert = jnp.minimum(jnp.sum((pad_ends[None, :] <= blk_ids[:, None]).astype(jnp.int32), axis=1),
                                 N_EXPERTS - 1)
        n_used = pad_ends[-1:].astype(jnp.int32)
        xpad = _dispatch(pad_ends.astype(jnp.int32), n_used, dest, hn2.reshape(t * SUBLANES, LANES), rows)
        ypad = _experts(blk_expert, n_used, xpad, w_expert_gate[l], w_expert_up[l], w_expert_down[l])
        x = _combine(dest, ypad, x1, ri, mod, g_final.reshape(1, d), l == w_ada.shape[0] - 1)
    return x
```

```python
import functools

import jax
import jax.numpy as jnp
from jax import lax
from jax.experimental import pallas as pl
from jax.experimental.pallas import tpu as pltpu

F32 = jnp.float32
BF16 = jnp.bfloat16

MLSTM_HEADS = 4
MLSTM_HEAD_DIM = 128
CONV_WIDTH = 4
MOBA_HEADS = 8
MOBA_HEAD_DIM = 64
MOBA_BLOCK = 256
MOBA_TOPK = 3
N_GROUPS = 4
EXPERTS_PER_GROUP = 8
N_EXPERTS = N_GROUPS * EXPERTS_PER_GROUP
EPS = 1e-6

LANES = 128
SUBLANES = 8
ROW_BLOCK = 512
MLSTM_CHUNK = 256
TOKEN_TILE = 512
COMBINE_TILE = 256
ROW_DMA_UNROLL = 8
NEG = -1e30
LOG2E = 1.4426950408889634
MOBA_GROUP = 2
MOBA_GROUP_KEYS = MOBA_GROUP * MOBA_BLOCK
MOBA_STEP_HEADS = 4
MOBA_STEP_GROUPS = 2
VMEM_LIMIT = 48 * 1024 * 1024


def _dot(a, b):
    return jnp.dot(a, b, preferred_element_type=F32)


def _dot_nt(a, b):
    return lax.dot_general(a, b, (((1,), (1,)), ((), ())), preferred_element_type=F32)


def _dot_tn(a, b):
    return lax.dot_general(a, b, (((0,), (0,)), ((), ())), preferred_element_type=F32)


def _sigmoid(x):
    return 1.0 / (1.0 + jnp.exp(-x))


def _log_sigmoid(x):
    return jnp.minimum(x, 0.0) - jnp.log1p(jnp.exp(-jnp.abs(x)))


def _split3(x):
    hi = x.astype(BF16)
    r1 = x - hi.astype(F32)
    mid = r1.astype(BF16)
    lo = (r1 - mid.astype(F32)).astype(BF16)
    return hi, mid, lo


def _iota(shape, dim):
    return lax.broadcasted_iota(jnp.int32, shape, dim)


def _load_rows(ref):
    n = ref.shape[0] // SUBLANES
    return jnp.concatenate([ref[pl.ds(j, n, stride=SUBLANES), :] for j in range(SUBLANES)], axis=1)


def _store_rows(ref, val):
    n = ref.shape[0] // SUBLANES
    for j in range(SUBLANES):
        ref[pl.ds(j, n, stride=SUBLANES), :] = val[:, j * LANES:(j + 1) * LANES]


def _token_tile(ref, t):
    return ref.at[pl.ds(pl.multiple_of(t * SUBLANES, SUBLANES), SUBLANES), :]


def _adaln_kernel(c_ref, w_ref, b_ref, o_ref):
    c = c_ref[...]
    o_ref[...] = _dot(c * _sigmoid(c), w_ref[...]) + b_ref[...]


def _adaln(c, w, b):
    bsz, d = c.shape
    n = w.shape[1]
    tn = 1024
    return pl.pallas_call(
        _adaln_kernel,
        out_shape=jax.ShapeDtypeStruct((bsz, n), F32),
        grid=(n // tn,),
        in_specs=[pl.BlockSpec((bsz, d), lambda j: (0, 0)),
                  pl.BlockSpec((d, tn), lambda j: (0, j)),
                  pl.BlockSpec((1, tn), lambda j: (0, j))],
        out_specs=pl.BlockSpec((bsz, tn), lambda j: (0, j)),
        name="adaln",
    )(c, w, b.reshape(1, n))


def _inproj_kernel(x_ref, shift_ref, scale_ref, g_ref, wm_ref, wa_ref, wvt_ref, wg_ref, wgt_ref, bg_ref, bgt_ref,
                   pm_ref, pa_ref, vt_ref, gc_ref, gt_ref):
    x = x_ref[...]
    ms = jnp.mean(x * x, axis=-1, keepdims=True)
    y = x * lax.rsqrt(ms + EPS) * g_ref[...]
    hb = (y * (1.0 + scale_ref[...]) + shift_ref[...]).astype(BF16)
    cw = 512
    for j in range(wm_ref.shape[1] // cw):
        pm_ref[:, j * cw:(j + 1) * cw] = _dot(hb, wm_ref[:, j * cw:(j + 1) * cw])
    for j in range(wa_ref.shape[1] // cw):
        pa_ref[:, j * cw:(j + 1) * cw] = _dot(hb, wa_ref[:, j * cw:(j + 1) * cw]).astype(BF16)
    vt = _dot_nt(wvt_ref[...], hb).astype(BF16)
    for j in range(vt_ref.shape[0]):
        vt_ref[j] = vt[:, j * MOBA_GROUP_KEYS:(j + 1) * MOBA_GROUP_KEYS]
    gc_ref[...] = _dot(hb, wg_ref[...]) + bg_ref[...]
    gt_ref[...] = _dot_nt(wgt_ref[...], hb) + bgt_ref[...]


def _inproj(x, mod, g1, wm, wa, wvt, wg, wgt, bg, bgt):
    bsz, s, d = x.shape
    tm = min(TOKEN_TILE, s)
    nm, na, nv = wm.shape[1], wa.shape[1], wvt.shape[0]
    vb = tm // MOBA_GROUP_KEYS
    ng = wgt.shape[0]
    const = lambda b, i: (0, 0)
    return pl.pallas_call(
        _inproj_kernel,
        out_shape=(jax.ShapeDtypeStruct((bsz, s, nm), F32),
                   jax.ShapeDtypeStruct((bsz, s, na), BF16),
                   jax.ShapeDtypeStruct((bsz, s // MOBA_GROUP_KEYS, nv, MOBA_GROUP_KEYS), BF16),
                   jax.ShapeDtypeStruct((bsz, s, LANES), F32),
                   jax.ShapeDtypeStruct((bsz, ng, s), F32)),
        grid=(bsz, s // tm),
        in_specs=[pl.BlockSpec((None, tm, d), lambda b, i: (b, i, 0)),
                  pl.BlockSpec((None, None, 1, d), lambda b, i: (b, 0, 0, 0)),
                  pl.BlockSpec((None, None, 1, d), lambda b, i: (b, 1, 0, 0)),
                  pl.BlockSpec((1, d), const),
                  pl.BlockSpec((d, nm), const),
                  pl.BlockSpec((d, na), const),
                  pl.BlockSpec((nv, d), const),
                  pl.BlockSpec((d, LANES), const),
                  pl.BlockSpec((ng, d), const),
                  pl.BlockSpec((1, LANES), const),
                  pl.BlockSpec((ng, 1), const)],
        out_specs=(pl.BlockSpec((None, tm, nm), lambda b, i: (b, i, 0)),
                   pl.BlockSpec((None, tm, na), lambda b, i: (b, i, 0)),
                   pl.BlockSpec((None, vb, nv, MOBA_GROUP_KEYS), lambda b, i: (b, i, 0, 0)),
                   pl.BlockSpec((None, tm, LANES), lambda b, i: (b, i, 0)),
                   pl.BlockSpec((None, ng, tm), lambda b, i: (b, 0, i))),
        compiler_params=pltpu.CompilerParams(dimension_semantics=("arbitrary", "arbitrary"),
                                             vmem_limit_bytes=VMEM_LIMIT),
        name="inproj",
    )(x, mod, mod, g1, wm, wa, wvt, wg, wgt, bg, bgt)


def _mlstm_kernel(q_ref, k_ref, v_ref, o_ref, gc_ref, gt_ref, wc_ref, bc_ref, gh_ref, out_ref,
                  qbuf, kbuf, c_sc, n_sc, m_sc):
    L = q_ref.shape[0]
    H, DH = MLSTM_HEADS, MLSTM_HEAD_DIM
    HD = H * DH
    ci = pl.program_id(1)

    @pl.when(ci == 0)
    def _():
        qbuf[0:SUBLANES, :] = jnp.zeros((SUBLANES, HD), F32)
        kbuf[0:SUBLANES, :] = jnp.zeros((SUBLANES, HD), F32)
        c_sc[...] = jnp.zeros_like(c_sc)
        n_sc[...] = jnp.zeros_like(n_sc)
        m_sc[...] = jnp.zeros_like(m_sc)

    qbuf[SUBLANES:SUBLANES + L, :] = q_ref[...]
    kbuf[SUBLANES:SUBLANES + L, :] = k_ref[...]
    qc = jnp.broadcast_to(bc_ref[:, 0:HD], (L, HD))
    kc = jnp.broadcast_to(bc_ref[:, HD:2 * HD], (L, HD))
    for j in range(CONV_WIDTH):
        off = SUBLANES - (CONV_WIDTH - 1) + j
        qc = qc + wc_ref[j:j + 1, 0:HD] * qbuf[off:off + L, :]
        kc = kc + wc_ref[j:j + 1, HD:2 * HD] * kbuf[off:off + L, :]
    qbuf[0:SUBLANES, :] = qbuf[L:L + SUBLANES, :]
    kbuf[0:SUBLANES, :] = kbuf[L:L + SUBLANES, :]
    qa = qc * _sigmoid(qc) * (DH ** -0.5)
    ka = kc * _sigmoid(kc)

    gc = gc_ref[...]
    gt = gt_ref[...]
    row = _iota((L, L), 0)
    col = _iota((L, L), 1)
    causal = col <= row
    tril = causal.astype(BF16)
    triu = (row <= col).astype(BF16)
    h3 = _split3(_log_sigmoid(gc))
    cum_c = _dot(tril, h3[0]) + _dot(tril, h3[1]) + _dot(tril, h3[2])
    r3 = _split3(_log_sigmoid(gt))
    cum_r = _dot(r3[0], triu) + _dot(r3[1], triu) + _dot(r3[2], triu)

    for h in range(H):
        sl = slice(h * DH, (h + 1) * DH)
        qf = qa[:, sl]
        qb = qf.astype(BF16)
        kf = ka[:, sl]
        kb = kf.astype(BF16)
        vb = v_ref[:, sl].astype(BF16)
        a_c = cum_c[:, H + h:H + h + 1]
        li_c = gc[:, h:h + 1]
        b_r = gt[h:h + 1, :] - cum_r[H + h:H + h + 1, :]
        cum_last = cum_r[H + h:H + h + 1, L - 1:L]
        m_prev = m_sc[h:h + 1, 0:1]
        c_prev = c_sc[h]
        n_prev = n_sc[h:h + 1, :]

        dm = jnp.where(causal, a_c + b_r, -jnp.inf)
        inter = a_c + m_prev
        m_t = jnp.maximum(inter, jnp.max(dm, axis=-1, keepdims=True))
        w_inter = jnp.exp(inter - m_t)
        sc = _dot_nt(qb, kb) * jnp.exp(dm - m_t)
        num = w_inter * _dot(qb, c_prev.astype(BF16)) + _dot(sc.astype(BF16), vb)
        den = w_inter * jnp.sum(qf * n_prev, axis=-1, keepdims=True) + jnp.sum(sc, axis=-1, keepdims=True)
        ht = num / jnp.maximum(jnp.abs(den), jnp.exp(-m_t))

        g_end_r = cum_last + b_r
        m_new = jnp.maximum(cum_last + m_prev, jnp.max(g_end_r, axis=-1, keepdims=True))
        w_old = jnp.exp(cum_last + m_prev - m_new)
        w_new_c = jnp.exp(cum_last + li_c - a_c - m_new)
        kw = kf * w_new_c
        c_sc[h] = w_old * c_prev + _dot_tn(kw.astype(BF16), vb)
        n_sc[h:h + 1, :] = w_old * n_prev + jnp.sum(kw, axis=0, keepdims=True)
        m_sc[h:h + 1, :] = jnp.broadcast_to(m_new, (1, LANES))

        hn = ht * lax.rsqrt(jnp.mean(ht * ht, axis=-1, keepdims=True) + EPS) * gh_ref[:, sl]
        out_ref[:, sl] = (hn * _sigmoid(o_ref[:, sl])).astype(BF16)


def _mlstm(pm, gc, gt, w_conv, b_conv, g_head):
    bsz, s, _ = pm.shape
    L = min(MLSTM_CHUNK, s)
    H, DH = MLSTM_HEADS, MLSTM_HEAD_DIM
    HD = H * DH
    const = lambda b, i: (0, 0)
    col_block = lambda j: pl.BlockSpec((None, L, HD), lambda b, i: (b, i, j))
    return pl.pallas_call(
        _mlstm_kernel,
        out_shape=jax.ShapeDtypeStruct((bsz, s, HD), BF16),
        grid=(bsz, s // L),
        in_specs=[col_block(0), col_block(1), col_block(2), col_block(3),
                  pl.BlockSpec((None, L, LANES), lambda b, i: (b, i, 0)),
                  pl.BlockSpec((None, 2 * H, L), lambda b, i: (b, 0, i)),
                  pl.BlockSpec((CONV_WIDTH, 2 * HD), const),
                  pl.BlockSpec((1, 2 * HD), const),
                  pl.BlockSpec((1, HD), const)],
        out_specs=pl.BlockSpec((None, L, HD), lambda b, i: (b, i, 0)),
        scratch_shapes=[pltpu.VMEM((L + SUBLANES, HD), F32),
                        pltpu.VMEM((L + SUBLANES, HD), F32),
                        pltpu.VMEM((H, DH, DH), F32),
                        pltpu.VMEM((SUBLANES, DH), F32),
                        pltpu.VMEM((SUBLANES, LANES), F32)],
        compiler_params=pltpu.CompilerParams(dimension_semantics=("arbitrary", "arbitrary"),
                                             vmem_limit_bytes=VMEM_LIMIT),
        name="mlstm",
    )(pm, pm, pm, pm, gc, gt, w_conv, b_conv.reshape(1, 2 * HD), g_head.reshape(1, HD))


def _moba_kernel(q_ref, k_ref, vt_ref, out_ref, km_sc, bias_sc, sa_sc, sb_sc):
    BS = MOBA_BLOCK
    DHA = MOBA_HEAD_DIM
    GB = MOBA_GROUP
    GK = GB * BS
    nb = k_ref.shape[0] // BS
    ngroups = nb // GB
    qi = pl.program_id(2)

    @pl.when(qi == 0)
    def _():
        def mean_body(j, carry):
            kj = k_ref[pl.ds(pl.multiple_of(j * BS, BS), BS), :].astype(F32)
            km_sc[pl.ds(j, 1), :] = jnp.mean(kj, axis=0, keepdims=True)
            return carry
        lax.fori_loop(0, nb, mean_body, 0)

    NH = sa_sc.shape[0]
    heads = range(NH)

    def pair_lanes(h):
        return slice((h // 2) * LANES, (h // 2 + 1) * LANES)

    lane = _iota((BS, LANES), 1)
    first = lane < DHA
    q = q_ref[...].astype(F32) * (DHA ** -0.5 * LOG2E)
    qh = [jnp.where(first == (h % 2 == 0), q[:, pair_lanes(h)], 0.0).astype(BF16) for h in heads]

    ones = jnp.ones((2 * SUBLANES, GK), BF16)

    def values(g, h):
        return jnp.concatenate([vt_ref[g, h * DHA:(h + 1) * DHA, :], ones], axis=0)

    def bias_row(h, g, c):
        return bias_sc[h, pl.ds(GB * g + c, 1), :]

    def scores(g, h, dst):
        gk = jnp.minimum(g, ngroups - 1)
        rows = pl.ds(pl.multiple_of(gk * GK, GK), GK)
        s = _dot_nt(k_ref[rows, pair_lanes(h)], qh[h])
        mx = []
        for c in range(GB):
            sc = s[c * BS:(c + 1) * BS]
            dst[h, c] = sc
            mx.append(jnp.max(sc, axis=0, keepdims=True))
        return mx

    def weights(g, h, src, mx, state):
        m_old, acc = state
        bias = [bias_row(h, g, c) for c in range(GB)]
        mn = m_old
        for c in range(GB):
            mn = jnp.maximum(mn, mx[c] + bias[c])
        p = [jnp.exp2(src[h, c] - (mn - bias[c])).astype(BF16) for c in range(GB)]
        pv = _dot(values(jnp.minimum(g, ngroups - 1), h), jnp.concatenate(p, axis=0))
        return mn, acc * jnp.exp2(m_old - mn) + pv

    kmb = km_sc[...].astype(BF16)
    gate = [_dot_nt(kmb[:, pair_lanes(h)], qh[h]) for h in heads]
    own = pl.ds(pl.multiple_of(qi * BS, BS), BS)
    s_own = [_dot_nt(k_ref[own, pair_lanes(h)], qh[h]) for h in heads]
    mx_a = [scores(0, h, sa_sc) for h in heads]

    blk = _iota((nb, BS), 0)
    blkf = blk.astype(F32)
    for h in heads:
        g = jnp.where(blk < qi, gate[h], -jnp.inf)
        picked = jnp.zeros((nb, BS), jnp.bool_)
        for _ in range(min(MOBA_TOPK, nb - 1)):
            mx = jnp.max(g, axis=0, keepdims=True)
            idx = jnp.min(jnp.where(g == mx, blkf, float(nb)), axis=0, keepdims=True)
            pick = blkf == idx
            picked = jnp.logical_or(picked, pick)
            g = jnp.where(pick, -jnp.inf, g)
        bias_sc[h, 0:nb] = jnp.where(jnp.logical_and(picked, blk < qi), 0.0, NEG)
        bias_sc[h, nb:nb + SUBLANES] = jnp.full((SUBLANES, BS), NEG, F32)

    key_pos = _iota((BS, BS), 0)
    query_pos = _iota((BS, BS), 1)
    causal = key_pos <= query_pos
    own_cols = pl.ds(pl.multiple_of((qi % GB) * BS, BS), BS)
    state = []
    for h in heads:
        s = jnp.where(causal, s_own[h], NEG)
        mn = jnp.max(s, axis=0, keepdims=True)
        p = jnp.exp2(s - mn).astype(BF16)
        v_own = jnp.concatenate([vt_ref[qi // GB, h * DHA:(h + 1) * DHA, own_cols], ones[:, 0:BS]], axis=0)
        state.append((mn, _dot(v_own, p)))

    NG = MOBA_STEP_GROUPS

    def past_step(t, carry):
        state, mx_a = carry
        for u in range(0, NG, 2):
            g = NG * t + u
            mx_b = [scores(g + 1, h, sb_sc) for h in heads]
            state = [weights(g, h, sa_sc, mx_a[h], state[h]) for h in heads]
            mx_a = [scores(g + 2, h, sa_sc) for h in heads]
            state = [weights(g + 1, h, sb_sc, mx_b[h], state[h]) for h in heads]
        return state, mx_a

    state, _ = lax.fori_loop(0, (qi + (NG * GB - 1)) // (NG * GB), past_step, (state, mx_a))
    out_t = jnp.concatenate([state[h][1][0:DHA] / state[h][1][DHA:DHA + 1] for h in heads], axis=0)
    out_ref[...] = out_t.T.astype(BF16)


def _moba(pa, vt):
    bsz, s, _ = pa.shape
    BS = MOBA_BLOCK
    nh = MOBA_STEP_HEADS
    w = nh * MOBA_HEAD_DIM
    nstep = MOBA_HEADS // nh
    nb = s // BS
    gb = MOBA_GROUP
    assert nb % gb == 0 and gb <= SUBLANES and w % LANES == 0
    return pl.pallas_call(
        _moba_kernel,
        out_shape=jax.ShapeDtypeStruct((bsz, s, nstep * w), BF16),
        grid=(bsz, nstep, nb),
        in_specs=[pl.BlockSpec((None, BS, w), lambda b, p, i: (b, i, p)),
                  pl.BlockSpec((None, s, w), lambda b, p, i: (b, 0, nstep + p)),
                  pl.BlockSpec((None, nb // gb, w, gb * BS), lambda b, p, i: (b, 0, p, 0))],
        out_specs=pl.BlockSpec((None, BS, w), lambda b, p, i: (b, i, p)),
        scratch_shapes=[pltpu.VMEM((nb, w), F32), pltpu.VMEM((nh, nb + SUBLANES, BS), F32),
                        pltpu.VMEM((nh, gb, BS, BS), F32), pltpu.VMEM((nh, gb, BS, BS), F32)],
        compiler_params=pltpu.CompilerParams(dimension_semantics=("arbitrary",) * 3,
                                             vmem_limit_bytes=VMEM_LIMIT),
        name="moba",
    )(pa, pa, vt)


def _outproj_kernel(hm_ref, ha_ref, x_ref, gate_ref, shift_ref, scale_ref, g_ref, wo_ref, wr_ref, br_ref,
                    x1_ref, hn_ref, ri_ref):
    dm = hm_ref.shape[1]
    mix = _dot(hm_ref[...], wo_ref[0:dm, :]) + _dot(ha_ref[...], wo_ref[dm:, :])
    x1 = x_ref[...] + gate_ref[...] * mix
    x1_ref[...] = x1
    ms = jnp.mean(x1 * x1, axis=-1, keepdims=True)
    hn = x1 * lax.rsqrt(ms + EPS) * g_ref[...] * (1.0 + scale_ref[...]) + shift_ref[...]
    _store_rows(hn_ref, hn)
    logits = _dot(hn.astype(BF16), wr_ref[...]) + br_ref[...]

    tm = logits.shape[0]
    lane = _iota((tm, LANES), 1)
    lanef = lane.astype(F32)
    big = float(LANES)
    in_groups = lane < N_GROUPS
    gl = jnp.where(in_groups, logits, -jnp.inf)
    gmax = jnp.max(gl, axis=-1, keepdims=True)
    gidx = jnp.min(jnp.where(gl == gmax, lanef, big), axis=-1, keepdims=True)
    g_w = 1.0 / jnp.sum(jnp.where(in_groups, jnp.exp(gl - gmax), 0.0), axis=-1, keepdims=True)
    lo = N_GROUPS + EXPERTS_PER_GROUP * gidx
    el = jnp.where(jnp.logical_and(lanef >= lo, lanef < lo + EXPERTS_PER_GROUP), logits, -jnp.inf)
    e1 = jnp.max(el, axis=-1, keepdims=True)
    i1 = jnp.min(jnp.where(el == e1, lanef, big), axis=-1, keepdims=True)
    el = jnp.where(lanef == i1, -jnp.inf, el)
    e2 = jnp.max(el, axis=-1, keepdims=True)
    i2 = jnp.min(jnp.where(el == e2, lanef, big), axis=-1, keepdims=True)
    t = jnp.exp(e2 - e1)
    w1 = g_w / (1.0 + t)
    w2 = g_w * t / (1.0 + t)
    ri = jnp.where(lane == 0, i1 - N_GROUPS,
                   jnp.where(lane == 1, i2 - N_GROUPS,
                             jnp.where(lane == 2, w1, jnp.where(lane == 3, w2, 0.0))))
    ri_ref[...] = ri


def _outproj(hm, ha, x, mod, g2, wo, wr, br):
    bsz, s, d = x.shape
    tm = min(TOKEN_TILE, s)
    dh = hm.shape[2]
    const = lambda b, i: (0, 0)
    tile = lambda w: pl.BlockSpec((None, tm, w), lambda b, i: (b, i, 0))
    modrow = lambda k: pl.BlockSpec((None, None, 1, d), lambda b, i: (b, k, 0, 0))
    return pl.pallas_call(
        _outproj_kernel,
        out_shape=(jax.ShapeDtypeStruct((bsz, s, d), F32),
                   jax.ShapeDtypeStruct((bsz, s * SUBLANES, LANES), F32),
                   jax.ShapeDtypeStruct((bsz, s, LANES), F32)),
        grid=(bsz, s // tm),
        in_specs=[tile(dh), tile(dh), tile(d), modrow(2), modrow(3), modrow(4),
                  pl.BlockSpec((1, d), const),
                  pl.BlockSpec((2 * dh, d), const),
                  pl.BlockSpec((d, LANES), const),
                  pl.BlockSpec((1, LANES), const)],
        out_specs=(tile(d), pl.BlockSpec((None, tm * SUBLANES, LANES), lambda b, i: (b, i, 0)), tile(LANES)),
        compiler_params=pltpu.CompilerParams(dimension_semantics=("arbitrary", "arbitrary"),
                                             vmem_limit_bytes=VMEM_LIMIT),
        name="outproj",
    )(hm, ha, x, mod, mod, mod, g2, wo, wr, br)


def _rank_kernel(ri_ref, dest_ref, cnt_ref, cnt_sc, run_sc):
    ph = pl.program_id(0)
    i = pl.program_id(1)
    tm = ri_ref.shape[0]
    lane = _iota((tm, LANES), 1).astype(F32)
    ri = ri_ref[...]
    oh0 = lane == ri[:, 0:1]
    oh1 = lane == ri[:, 1:2]
    oh = jnp.where(jnp.logical_or(oh0, oh1), 1.0, 0.0)
    colsum = jnp.sum(oh, axis=0, keepdims=True)

    @pl.when(jnp.logical_and(ph == 0, i == 0))
    def _():
        cnt_sc[...] = jnp.zeros_like(cnt_sc)

    @pl.when(ph == 0)
    def _():
        cnt_sc[...] = cnt_sc[...] + colsum
        dest_ref[...] = jnp.zeros_like(dest_ref)
        cnt_ref[...] = cnt_sc[...]

    @pl.when(jnp.logical_and(ph == 1, i == 0))
    def _():
        padded = jnp.ceil(cnt_sc[...] / ROW_BLOCK) * ROW_BLOCK
        l8 = _iota((SUBLANES, LANES), 1)
        acc = padded
        k = 1
        while k < LANES:
            acc = acc + jnp.where(l8 >= k, pltpu.roll(acc, k, 1), 0.0)
            k *= 2
        run_sc[...] = acc - padded

    @pl.when(ph == 1)
    def _():
        row = _iota((tm, tm), 0)
        col = _iota((tm, tm), 1)
        before = _dot((col < row).astype(BF16), oh.astype(BF16))
        tot = before + run_sc[0:1, :]
        d0 = jnp.sum(jnp.where(oh0, tot, 0.0), axis=-1, keepdims=True)
        d1 = jnp.sum(jnp.where(oh1, tot, 0.0), axis=-1, keepdims=True)
        lane_i = _iota((tm, LANES), 1)
        dest_ref[...] = jnp.where(lane_i == 0, d0, jnp.where(lane_i == 1, d1, 0.0))
        run_sc[...] = run_sc[...] + colsum
        cnt_ref[...] = cnt_sc[...]


def _rank(ri):
    t = ri.shape[0]
    tm = min(TOKEN_TILE, t)
    return pl.pallas_call(
        _rank_kernel,
        out_shape=(jax.ShapeDtypeStruct((t, LANES), F32),
                   jax.ShapeDtypeStruct((SUBLANES, LANES), F32)),
        grid=(2, t // tm),
        in_specs=[pl.BlockSpec((tm, LANES), lambda p, i: (i, 0))],
        out_specs=(pl.BlockSpec((tm, LANES), lambda p, i: (i * p, 0)),
                   pl.BlockSpec((SUBLANES, LANES), lambda p, i: (0, 0))),
        scratch_shapes=[pltpu.VMEM((SUBLANES, LANES), F32), pltpu.VMEM((SUBLANES, LANES), F32)],
        compiler_params=pltpu.CompilerParams(dimension_semantics=("arbitrary", "arbitrary")),
        name="rank",
    )(ri)


def _dispatch_kernel(pe_ref, nu_ref, dest_ref, hn_ref, xpad_ref, zbuf, sem, zsem):
    tm = hn_ref.shape[0] // SUBLANES
    block_rows = ROW_BLOCK * SUBLANES
    nblk = xpad_ref.shape[0] // block_rows

    def zero_block(j):
        rows = pl.ds(pl.multiple_of(j * block_rows, block_rows), block_rows)
        return pltpu.make_async_copy(zbuf, xpad_ref.at[rows, :], zsem)

    @pl.when(pl.program_id(0) == 0)
    def _():
        zbuf[...] = jnp.zeros_like(zbuf)

        def tail(e, n):
            first = jnp.where(e == 0, 0, pe_ref[jnp.maximum(e - 1, 0)])
            nonempty = pe_ref[e] > first

            @pl.when(nonempty)
            def _():
                zero_block(pe_ref[e] - 1).start()
            return n + nonempty.astype(jnp.int32)
        started = lax.fori_loop(0, N_EXPERTS, tail, 0)

        def unused(j, c):
            zero_block(j).start()
            return c
        lax.fori_loop(nu_ref[0], nblk, unused, 0)

        def drain(j, c):
            zero_block(0).wait()
            return c
        lax.fori_loop(0, started + nblk - nu_ref[0], drain, 0)

    def row_copy(t, k):
        d = dest_ref[0, 2 * t + k]
        return pltpu.make_async_copy(_token_tile(hn_ref, t), _token_tile(xpad_ref, d), sem)

    def issue(t, carry):
        row_copy(t, 0).start(priority=0)
        row_copy(t, 1).start(priority=1)
        return carry
    lax.fori_loop(0, tm, issue, 0, unroll=ROW_DMA_UNROLL)

    for _ in range(2):
        pltpu.make_async_copy(hn_ref, xpad_ref.at[pl.ds(0, tm * SUBLANES), :], sem).wait()


def _dispatch(pad_ends, n_used, dest, hn, rows):
    t = hn.shape[0] // SUBLANES
    tm = min(TOKEN_TILE, t)
    return pl.pallas_call(
        _dispatch_kernel,
        out_shape=jax.ShapeDtypeStruct((rows * SUBLANES, LANES), hn.dtype),
        grid_spec=pltpu.PrefetchScalarGridSpec(
            num_scalar_prefetch=2,
            grid=(t // tm,),
            in_specs=[pl.BlockSpec((None, 1, 2 * tm), lambda i, pe, nu: (i, 0, 0), memory_space=pltpu.SMEM),
                      pl.BlockSpec((tm * SUBLANES, LANES), lambda i, pe, nu: (i, 0))],
            out_specs=pl.BlockSpec(memory_space=pl.ANY),
            scratch_shapes=[pltpu.VMEM((ROW_BLOCK * SUBLANES, LANES), hn.dtype),
                            pltpu.SemaphoreType.DMA, pltpu.SemaphoreType.DMA]),
        compiler_params=pltpu.CompilerParams(dimension_semantics=("arbitrary",)),
        name="dispatch",
    )(pad_ends, n_used, dest.reshape(t // tm, 1, 2 * tm), hn)


def _expert_kernel(be_ref, nu_ref, x_ref, wg_ref, wu_ref, wd_ref, y_ref, wgb, wub, wdb):
    i = pl.program_id(0)
    live = i < nu_ref[0]
    new_expert = jnp.logical_or(i == 0, be_ref[i] != be_ref[jnp.maximum(i - 1, 0)])

    @pl.when(jnp.logical_and(live, new_expert))
    def _():
        wgb[...] = wg_ref[...].astype(BF16)
        wub[...] = wu_ref[...].astype(BF16)
        wdb[...] = wd_ref[...].astype(BF16)

    @pl.when(live)
    def _():
        xb = _load_rows(x_ref).astype(BF16)
        g = _dot(xb, wgb[...])
        u = _dot(xb, wub[...])
        _store_rows(y_ref, _dot((g * _sigmoid(g) * u).astype(BF16), wdb[...]))

    @pl.when(jnp.logical_not(live))
    def _():
        y_ref[...] = jnp.zeros_like(y_ref)


def _experts(blk_expert, n_used, xpad, wg, wu, wd):
    rows = xpad.shape[0] // SUBLANES
    d, de = wg.shape[1], wg.shape[2]
    assert d == SUBLANES * LANES
    row_block = (ROW_BLOCK * SUBLANES, LANES)
    nblk = rows // ROW_BLOCK
    live = lambda i, be, nu: jnp.minimum(i, nu[0] - 1)
    return pl.pallas_call(
        _expert_kernel,
        out_shape=jax.ShapeDtypeStruct(xpad.shape, F32),
        grid_spec=pltpu.PrefetchScalarGridSpec(
            num_scalar_prefetch=2,
            grid=(nblk,),
            in_specs=[pl.BlockSpec(row_block, lambda i, be, nu: (live(i, be, nu), 0)),
                      pl.BlockSpec((None, d, de), lambda i, be, nu: (be[live(i, be, nu)], 0, 0)),
                      pl.BlockSpec((None, d, de), lambda i, be, nu: (be[live(i, be, nu)], 0, 0)),
                      pl.BlockSpec((None, de, d), lambda i, be, nu: (be[live(i, be, nu)], 0, 0))],
            out_specs=pl.BlockSpec(row_block, lambda i, be, nu: (i, 0)),
            scratch_shapes=[pltpu.VMEM((d, de), BF16), pltpu.VMEM((d, de), BF16), pltpu.VMEM((de, d), BF16)]),
        compiler_params=pltpu.CompilerParams(dimension_semantics=("arbitrary",),
                                             vmem_limit_bytes=VMEM_LIMIT),
        name="experts",
    )(blk_expert, n_used, xpad, wg, wu, wd)


def _combine_kernel(final_norm, dest_ref, dnext_ref, ypad_ref, x1_ref, ri_ref, gate_ref, gf_ref, out_ref, ybuf, sem):
    tm = x1_ref.shape[0]
    step = pl.program_id(0) * pl.num_programs(1) + pl.program_id(1)
    nsteps = pl.num_programs(0) * pl.num_programs(1)
    slot = step % 2

    def gather(idx_ref, buf):
        def row_copy(t, k):
            d = idx_ref[0, 2 * t + k]
            return pltpu.make_async_copy(_token_tile(ypad_ref, d), _token_tile(ybuf.at[buf, k], t), sem.at[buf])

        def issue(t, carry):
            row_copy(t, 0).start(priority=0)
            row_copy(t, 1).start(priority=1)
            return carry
        lax.fori_loop(0, tm, issue, 0, unroll=ROW_DMA_UNROLL)

    @pl.when(step == 0)
    def _():
        gather(dest_ref, 0)

    @pl.when(step + 1 < nsteps)
    def _():
        gather(dnext_ref, 1 - slot)

    for k in range(2):
        pltpu.make_async_copy(ypad_ref.at[pl.ds(0, tm * SUBLANES), :], ybuf.at[slot, k], sem.at[slot]).wait()

    ri = ri_ref[...]
    ffn = ri[:, 2:3] * _load_rows(ybuf.at[slot, 0]) + ri[:, 3:4] * _load_rows(ybuf.at[slot, 1])
    x2 = x1_ref[...] + gate_ref[...] * ffn
    if final_norm:
        ms = jnp.mean(x2 * x2, axis=-1, keepdims=True)
        x2 = x2 * lax.rsqrt(ms + EPS) * gf_ref[...]
    out_ref[...] = x2


def _combine(dest, ypad, x1, ri, mod, gf, final_norm):
    bsz, s, d = x1.shape
    tm = min(COMBINE_TILE, s)
    nt = s // tm
    dest3 = dest.reshape(bsz * nt, 1, 2 * tm)
    return pl.pallas_call(
        functools.partial(_combine_kernel, final_norm),
        out_shape=jax.ShapeDtypeStruct((bsz, s, d), F32),
        grid=(bsz, nt),
        in_specs=[pl.BlockSpec((None, 1, 2 * tm), lambda b, i: (b * nt + i, 0, 0), memory_space=pltpu.SMEM),
                  pl.BlockSpec((None, 1, 2 * tm), lambda b, i: (jnp.minimum(b * nt + i + 1, bsz * nt - 1), 0, 0),
                               memory_space=pltpu.SMEM),
                  pl.BlockSpec(memory_space=pl.ANY),
                  pl.BlockSpec((None, tm, d), lambda b, i: (b, i, 0)),
                  pl.BlockSpec((None, tm, LANES), lambda b, i: (b, i, 0)),
                  pl.BlockSpec((None, None, 1, d), lambda b, i: (b, 5, 0, 0)),
                  pl.BlockSpec((1, d), lambda b, i: (0, 0))],
        out_specs=pl.BlockSpec((None, tm, d), lambda b, i: (b, i, 0)),
        scratch_shapes=[pltpu.VMEM((2, 2, tm * SUBLANES, LANES), F32), pltpu.SemaphoreType.DMA((2,))],
        compiler_params=pltpu.CompilerParams(dimension_semantics=("arbitrary", "arbitrary"),
                                             vmem_limit_bytes=VMEM_LIMIT),
        name="combine",
    )(dest3, dest3, ypad, x1, ri.reshape(bsz, s, LANES), mod, gf)


def _pad_lanes(a):
    return jnp.pad(a, ((0, 0), (0, LANES - a.shape[1])))


def kernel(x, c, w_ada, b_ada, g_norm1, w_in, w_conv, b_conv, b_gates, g_mlstm_head, w_out, g_norm2,
           w_router_group, b_router_group, w_router_expert, b_router_expert, w_expert_gate, w_expert_up,
           w_expert_down, g_final):
    bsz, s, d = x.shape
    t = bsz * s
    dm = MLSTM_HEADS * MLSTM_HEAD_DIM
    da = MOBA_HEADS * MOBA_HEAD_DIM
    ng = 2 * MLSTM_HEADS
    n_assign = 2 * t
    n_blocks = -(-n_assign // ROW_BLOCK) + N_EXPERTS
    rows = n_blocks * ROW_BLOCK

    for l in range(w_ada.shape[0]):
        mod = _adaln(c, w_ada[l], b_ada[l]).reshape(bsz, 6, 1, d)

        w = w_in[l]
        wm = w[:, :4 * dm].astype(BF16)
        wgates = w[:, 4 * dm:4 * dm + ng]
        wa = w[:, 4 * dm + ng:4 * dm + ng + 2 * da].astype(BF16)
        wvt = w[:, 4 * dm + ng + 2 * da:].T.astype(BF16)
        pm, pa, vt, gc, gt = _inproj(x, mod, g_norm1[l].reshape(1, d), wm, wa, wvt,
                                     _pad_lanes(wgates).astype(BF16), wgates.T.astype(BF16),
                                     _pad_lanes(b_gates[l].reshape(1, ng)), b_gates[l].reshape(ng, 1))
        hm = _mlstm(pm, gc, gt, w_conv[l], b_conv[l], g_mlstm_head[l])
        ha = _moba(pa, vt)

        wr = _pad_lanes(jnp.concatenate([w_router_group[l], w_router_expert[l]], axis=1)).astype(BF16)
        br = _pad_lanes(jnp.concatenate([b_router_group[l], b_router_expert[l]]).reshape(1, -1))
        x1, hn2, ri = _outproj(hm, ha, x, mod, g_norm2[l].reshape(1, d), w_out[l].astype(BF16), wr, br)
        ri = ri.reshape(t, LANES)
        destf, cnt = _rank(ri)
        dest = destf[:, :2].astype(jnp.int32).reshape(n_assign)
        counts = cnt[0, :N_EXPERTS].astype(jnp.int32)
        pad_ends = jnp.cumsum((counts + ROW_BLOCK - 1) // ROW_BLOCK)
        blk_ids = jnp.arange(n_blocks, dtype=jnp.int32)
        blk_expert = jnp.minimum(jnp.sum((pad_ends[None, :] <= blk_ids[:, None]).astype(jnp.int32), axis=1),
                                 N_EXPERTS - 1)
        n_used = pad_ends[-1:].astype(jnp.int32)
        xpad = _dispatch(pad_ends.astype(jnp.int32), n_used, dest, hn2.reshape(t * SUBLANES, LANES), rows)
        ypad = _experts(blk_expert, n_used, xpad, w_expert_gate[l], w_expert_up[l], w_expert_down[l])
        x = _combine(dest, ypad, x1, ri, mod, g_final.reshape(1, d), l == w_ada.shape[0] - 1)
    return x
```

```python
import functools

import jax
import jax.numpy as jnp
from jax import lax
from jax.experimental import pallas as pl
from jax.experimental.pallas import tpu as pltpu

F32 = jnp.float32
BF16 = jnp.bfloat16

MLSTM_HEADS = 4
MLSTM_HEAD_DIM = 128
CONV_WIDTH = 4
MOBA_HEADS = 8
MOBA_HEAD_DIM = 64
MOBA_BLOCK = 256
MOBA_TOPK = 3
N_GROUPS = 4
EXPERTS_PER_GROUP = 8
N_EXPERTS = N_GROUPS * EXPERTS_PER_GROUP
EPS = 1e-6

LANES = 128
SUBLANES = 8
ROW_BLOCK = 512
MLSTM_CHUNK = 256
TOKEN_TILE = 512
COMBINE_TILE = 256
ROW_DMA_UNROLL = 8
NEG = -1e30
LOG2E = 1.4426950408889634
MOBA_GROUP = 2
MOBA_GROUP_KEYS = MOBA_GROUP * MOBA_BLOCK
MOBA_STEP_HEADS = 4
MOBA_STEP_GROUPS = 2
VMEM_LIMIT = 48 * 1024 * 1024


def _dot(a, b):
    return jnp.dot(a, b, preferred_element_type=F32)


def _dot_nt(a, b):
    return lax.dot_general(a, b, (((1,), (1,)), ((), ())), preferred_element_type=F32)


def _dot_tn(a, b):
    return lax.dot_general(a, b, (((0,), (0,)), ((), ())), preferred_element_type=F32)


def _sigmoid(x):
    return 1.0 / (1.0 + jnp.exp(-x))


def _log_sigmoid(x):
    return jnp.minimum(x, 0.0) - jnp.log1p(jnp.exp(-jnp.abs(x)))


def _split3(x):
    hi = x.astype(BF16)
    r1 = x - hi.astype(F32)
    mid = r1.astype(BF16)
    lo = (r1 - mid.astype(F32)).astype(BF16)
    return hi, mid, lo


def _iota(shape, dim):
    return lax.broadcasted_iota(jnp.int32, shape, dim)


def _load_rows(ref):
    n = ref.shape[0] // SUBLANES
    return jnp.concatenate([ref[pl.ds(j, n, stride=SUBLANES), :] for j in range(SUBLANES)], axis=1)


def _store_rows(ref, val):
    n = ref.shape[0] // SUBLANES
    for j in range(SUBLANES):
        ref[pl.ds(j, n, stride=SUBLANES), :] = val[:, j * LANES:(j + 1) * LANES]


def _token_tile(ref, t):
    return ref.at[pl.ds(pl.multiple_of(t * SUBLANES, SUBLANES), SUBLANES), :]


def _adaln_kernel(c_ref, w_ref, b_ref, o_ref):
    c = c_ref[...]
    o_ref[...] = _dot(c * _sigmoid(c), w_ref[...]) + b_ref[...]


def _adaln(c, w, b):
    bsz, d = c.shape
    n = w.shape[1]
    tn = 1024
    return pl.pallas_call(
        _adaln_kernel,
        out_shape=jax.ShapeDtypeStruct((bsz, n), F32),
        grid=(n // tn,),
        in_specs=[pl.BlockSpec((bsz, d), lambda j: (0, 0)),
                  pl.BlockSpec((d, tn), lambda j: (0, j)),
                  pl.BlockSpec((1, tn), lambda j: (0, j))],
        out_specs=pl.BlockSpec((bsz, tn), lambda j: (0, j)),
        name="adaln",
    )(c, w, b.reshape(1, n))


def _inproj_kernel(x_ref, shift_ref, scale_ref, g_ref, wm_ref, wa_ref, wvt_ref, wg_ref, wgt_ref, bg_ref, bgt_ref,
                   pm_ref, pa_ref, vt_ref, gc_ref, gt_ref):
    x = x_ref[...]
    ms = jnp.mean(x * x, axis=-1, keepdims=True)
    y = x * lax.rsqrt(ms + EPS) * g_ref[...]
    hb = (y * (1.0 + scale_ref[...]) + shift_ref[...]).astype(BF16)
    cw = 512
    for j in range(wm_ref.shape[1] // cw):
        pm_ref[:, j * cw:(j + 1) * cw] = _dot(hb, wm_ref[:, j * cw:(j + 1) * cw])
    for j in range(wa_ref.shape[1] // cw):
        pa_ref[:, j * cw:(j + 1) * cw] = _dot(hb, wa_ref[:, j * cw:(j + 1) * cw]).astype(BF16)
    vt = _dot_nt(wvt_ref[...], hb).astype(BF16)
    for j in range(vt_ref.shape[0]):
        vt_ref[j] = vt[:, j * MOBA_GROUP_KEYS:(j + 1) * MOBA_GROUP_KEYS]
    gc_ref[...] = _dot(hb, wg_ref[...]) + bg_ref[...]
    gt_ref[...] = _dot_nt(wgt_ref[...], hb) + bgt_ref[...]


def _inproj(x, mod, g1, wm, wa, wvt, wg, wgt, bg, bgt):
    bsz, s, d = x.shape
    tm = min(TOKEN_TILE, s)
    nm, na, nv = wm.shape[1], wa.shape[1], wvt.shape[0]
    vb = tm // MOBA_GROUP_KEYS
    ng = wgt.shape[0]
    const = lambda b, i: (0, 0)
    return pl.pallas_call(
        _inproj_kernel,
        out_shape=(jax.ShapeDtypeStruct((bsz, s, nm), F32),
                   jax.ShapeDtypeStruct((bsz, s, na), BF16),
                   jax.ShapeDtypeStruct((bsz, s // MOBA_GROUP_KEYS, nv, MOBA_GROUP_KEYS), BF16),
                   jax.ShapeDtypeStruct((bsz, s, LANES), F32),
                   jax.ShapeDtypeStruct((bsz, ng, s), F32)),
        grid=(bsz, s // tm),
        in_specs=[pl.BlockSpec((None, tm, d), lambda b, i: (b, i, 0)),
                  pl.BlockSpec((None, None, 1, d), lambda b, i: (b, 0, 0, 0)),
                  pl.BlockSpec((None, None, 1, d), lambda b, i: (b, 1, 0, 0)),
                  pl.BlockSpec((1, d), const),
                  pl.BlockSpec((d, nm), const),
                  pl.BlockSpec((d, na), const),
                  pl.BlockSpec((nv, d), const),
                  pl.BlockSpec((d, LANES), const),
                  pl.BlockSpec((ng, d), const),
                  pl.BlockSpec((1, LANES), const),
                  pl.BlockSpec((ng, 1), const)],
        out_specs=(pl.BlockSpec((None, tm, nm), lambda b, i: (b, i, 0)),
                   pl.BlockSpec((None, tm, na), lambda b, i: (b, i, 0)),
                   pl.BlockSpec((None, vb, nv, MOBA_GROUP_KEYS), lambda b, i: (b, i, 0, 0)),
                   pl.BlockSpec((None, tm, LANES), lambda b, i: (b, i, 0)),
                   pl.BlockSpec((None, ng, tm), lambda b, i: (b, 0, i))),
        compiler_params=pltpu.CompilerParams(dimension_semantics=("arbitrary", "arbitrary"),
                                             vmem_limit_bytes=VMEM_LIMIT),
        name="inproj",
    )(x, mod, mod, g1, wm, wa, wvt, wg, wgt, bg, bgt)


def _mlstm_kernel(q_ref, k_ref, v_ref, o_ref, gc_ref, gt_ref, wc_ref, bc_ref, gh_ref, out_ref,
                  qbuf, kbuf, c_sc, n_sc, m_sc):
    L = q_ref.shape[0]
    H, DH = MLSTM_HEADS, MLSTM_HEAD_DIM
    HD = H * DH
    ci = pl.program_id(1)

    @pl.when(ci == 0)
    def _():
        qbuf[0:SUBLANES, :] = jnp.zeros((SUBLANES, HD), F32)
        kbuf[0:SUBLANES, :] = jnp.zeros((SUBLANES, HD), F32)
        c_sc[...] = jnp.zeros_like(c_sc)
        n_sc[...] = jnp.zeros_like(n_sc)
        m_sc[...] = jnp.zeros_like(m_sc)

    qbuf[SUBLANES:SUBLANES + L, :] = q_ref[...]
    kbuf[SUBLANES:SUBLANES + L, :] = k_ref[...]
    qc = jnp.broadcast_to(bc_ref[:, 0:HD], (L, HD))
    kc = jnp.broadcast_to(bc_ref[:, HD:2 * HD], (L, HD))
    for j in range(CONV_WIDTH):
        off = SUBLANES - (CONV_WIDTH - 1) + j
        qc = qc + wc_ref[j:j + 1, 0:HD] * qbuf[off:off + L, :]
        kc = kc + wc_ref[j:j + 1, HD:2 * HD] * kbuf[off:off + L, :]
    qbuf[0:SUBLANES, :] = qbuf[L:L + SUBLANES, :]
    kbuf[0:SUBLANES, :] = kbuf[L:L + SUBLANES, :]
    qa = qc * _sigmoid(qc) * (DH ** -0.5)
    ka = kc * _sigmoid(kc)

    gc = gc_ref[...]
    gt = gt_ref[...]
    row = _iota((L, L), 0)
    col = _iota((L, L), 1)
    causal = col <= row
    tril = causal.astype(BF16)
    triu = (row <= col).astype(BF16)
    h3 = _split3(_log_sigmoid(gc))
    cum_c = _dot(tril, h3[0]) + _dot(tril, h3[1]) + _dot(tril, h3[2])
    r3 = _split3(_log_sigmoid(gt))
    cum_r = _dot(r3[0], triu) + _dot(r3[1], triu) + _dot(r3[2], triu)

    for h in range(H):
        sl = slice(h * DH, (h + 1) * DH)
        qf = qa[:, sl]
        qb = qf.astype(BF16)
        kf = ka[:, sl]
        kb = kf.astype(BF16)
        vb = v_ref[:, sl].astype(BF16)
        a_c = cum_c[:, H + h:H + h + 1]
        li_c = gc[:, h:h + 1]
        b_r = gt[h:h + 1, :] - cum_r[H + h:H + h + 1, :]
        cum_last = cum_r[H + h:H + h + 1, L - 1:L]
        m_prev = m_sc[h:h + 1, 0:1]
        c_prev = c_sc[h]
        n_prev = n_sc[h:h + 1, :]

        dm = jnp.where(causal, a_c + b_r, -jnp.inf)
        inter = a_c + m_prev
        m_t = jnp.maximum(inter, jnp.max(dm, axis=-1, keepdims=True))
        w_inter = jnp.exp(inter - m_t)
        sc = _dot_nt(qb, kb) * jnp.exp(dm - m_t)
        num = w_inter * _dot(qb, c_prev.astype(BF16)) + _dot(sc.astype(BF16), vb)
        den = w_inter * jnp.sum(qf * n_prev, axis=-1, keepdims=True) + jnp.sum(sc, axis=-1, keepdims=True)
        ht = num / jnp.maximum(jnp.abs(den), jnp.exp(-m_t))

        g_end_r = cum_last + b_r
        m_new = jnp.maximum(cum_last + m_prev, jnp.max(g_end_r, axis=-1, keepdims=True))
        w_old = jnp.exp(cum_last + m_prev - m_new)
        w_new_c = jnp.exp(cum_last + li_c - a_c - m_new)
        kw = kf * w_new_c
        c_sc[h] = w_old * c_prev + _dot_tn(kw.astype(BF16), vb)
        n_sc[h:h + 1, :] = w_old * n_prev + jnp.sum(kw, axis=0, keepdims=True)
        m_sc[h:h + 1, :] = jnp.broadcast_to(m_new, (1, LANES))

        hn = ht * lax.rsqrt(jnp.mean(ht * ht, axis=-1, keepdims=True) + EPS) * gh_ref[:, sl]
        out_ref[:, sl] = (hn * _sigmoid(o_ref[:, sl])).astype(BF16)


def _mlstm(pm, gc, gt, w_conv, b_conv, g_head):
    bsz, s, _ = pm.shape
    L = min(MLSTM_CHUNK, s)
    H, DH = MLSTM_HEADS, MLSTM_HEAD_DIM
    HD = H * DH
    const = lambda b, i: (0, 0)
    col_block = lambda j: pl.BlockSpec((None, L, HD), lambda b, i: (b, i, j))
    return pl.pallas_call(
        _mlstm_kernel,
        out_shape=jax.ShapeDtypeStruct((bsz, s, HD), BF16),
        grid=(bsz, s // L),
        in_specs=[col_block(0), col_block(1), col_block(2), col_block(3),
                  pl.BlockSpec((None, L, LANES), lambda b, i: (b, i, 0)),
                  pl.BlockSpec((None, 2 * H, L), lambda b, i: (b, 0, i)),
                  pl.BlockSpec((CONV_WIDTH, 2 * HD), const),
                  pl.BlockSpec((1, 2 * HD), const),
                  pl.BlockSpec((1, HD), const)],
        out_specs=pl.BlockSpec((None, L, HD), lambda b, i: (b, i, 0)),
        scratch_shapes=[pltpu.VMEM((L + SUBLANES, HD), F32),
                        pltpu.VMEM((L + SUBLANES, HD), F32),
                        pltpu.VMEM((H, DH, DH), F32),
                        pltpu.VMEM((SUBLANES, DH), F32),
                        pltpu.VMEM((SUBLANES, LANES), F32)],
        compiler_params=pltpu.CompilerParams(dimension_semantics=("arbitrary", "arbitrary"),
                                             vmem_limit_bytes=VMEM_LIMIT),
        name="mlstm",
    )(pm, pm, pm, pm, gc, gt, w_conv, b_conv.reshape(1, 2 * HD), g_head.reshape(1, HD))


def _moba_kernel(q_ref, k_ref, vt_ref, out_ref, km_sc, bias_sc, sa_sc, sb_sc):
    BS = MOBA_BLOCK
    DHA = MOBA_HEAD_DIM
    GB = MOBA_GROUP
    GK = GB * BS
    nb = k_ref.shape[0] // BS
    ngroups = nb // GB
    qi = pl.program_id(2)

    @pl.when(qi == 0)
    def _():
        def mean_body(j, carry):
            kj = k_ref[pl.ds(pl.multiple_of(j * BS, BS), BS), :].astype(F32)
            km_sc[pl.ds(j, 1), :] = jnp.mean(kj, axis=0, keepdims=True)
            return carry
        lax.fori_loop(0, nb, mean_body, 0)

    NH = sa_sc.shape[0]
    heads = range(NH)

    def pair_lanes(h):
        return slice((h // 2) * LANES, (h // 2 + 1) * LANES)

    lane = _iota((BS, LANES), 1)
    first = lane < DHA
    q = q_ref[...].astype(F32) * (DHA ** -0.5 * LOG2E)
    qh = [jnp.where(first == (h % 2 == 0), q[:, pair_lanes(h)], 0.0).astype(BF16) for h in heads]

    ones = jnp.ones((2 * SUBLANES, GK), BF16)

    def values(g, h):
        return jnp.concatenate([vt_ref[g, h * DHA:(h + 1) * DHA, :], ones], axis=0)

    def bias_row(h, g, c):
        return bias_sc[h, pl.ds(GB * g + c, 1), :]

    def scores(g, h, dst):
        gk = jnp.minimum(g, ngroups - 1)
        rows = pl.ds(pl.multiple_of(gk * GK, GK), GK)
        s = _dot_nt(k_ref[rows, pair_lanes(h)], qh[h])
        mx = []
        for c in range(GB):
            sc = s[c * BS:(c + 1) * BS]
            dst[h, c] = sc
            mx.append(jnp.max(sc, axis=0, keepdims=True))
        return mx

    def weights(g, h, src, mx, state):
        m_old, acc = state
        bias = [bias_row(h, g, c) for c in range(GB)]
        mn = m_old
        for c in range(GB):
            mn = jnp.maximum(mn, mx[c] + bias[c])
        p = [jnp.exp2(src[h, c] - (mn - bias[c])).astype(BF16) for c in range(GB)]
        pv = _dot(values(jnp.minimum(g, ngroups - 1), h), jnp.concatenate(p, axis=0))
        return mn, acc * jnp.exp2(m_old - mn) + pv

    kmb = km_sc[...].astype(BF16)
    gate = [_dot_nt(kmb[:, pair_lanes(h)], qh[h]) for h in heads]
    own = pl.ds(pl.multiple_of(qi * BS, BS), BS)
    s_own = [_dot_nt(k_ref[own, pair_lanes(h)], qh[h]) for h in heads]
    mx_a = [scores(0, h, sa_sc) for h in heads]

    blk = _iota((nb, BS), 0)
    blkf = blk.astype(F32)
    for h in heads:
        g = jnp.where(blk < qi, gate[h], -jnp.inf)
        picked = jnp.zeros((nb, BS), jnp.bool_)
        for _ in range(min(MOBA_TOPK, nb - 1)):
            mx = jnp.max(g, axis=0, keepdims=True)
            idx = jnp.min(jnp.where(g == mx, blkf, float(nb)), axis=0, keepdims=True)
            pick = blkf == idx
            picked = jnp.logical_or(picked, pick)
            g = jnp.where(pick, -jnp.inf, g)
        bias_sc[h, 0:nb] = jnp.where(jnp.logical_and(picked, blk < qi), 0.0, NEG)
        bias_sc[h, nb:nb + SUBLANES] = jnp.full((SUBLANES, BS), NEG, F32)

    key_pos = _iota((BS, BS), 0)
    query_pos = _iota((BS, BS), 1)
    causal = key_pos <= query_pos
    own_cols = pl.ds(pl.multiple_of((qi % GB) * BS, BS), BS)
    state = []
    for h in heads:
        s = jnp.where(causal, s_own[h], NEG)
        mn = jnp.max(s, axis=0, keepdims=True)
        p = jnp.exp2(s - mn).astype(BF16)
        v_own = jnp.concatenate([vt_ref[qi // GB, h * DHA:(h + 1) * DHA, own_cols], ones[:, 0:BS]], axis=0)
        state.append((mn, _dot(v_own, p)))

    NG = MOBA_STEP_GROUPS

    def past_step(t, carry):
        state, mx_a = carry
        for u in range(0, NG, 2):
            g = NG * t + u
            mx_b, mx_c, state = list(mx_a), list(mx_a), list(state)
            for h in heads:
                mx_b[h] = scores(g + 1, h, sb_sc)
                state[h] = weights(g, h, sa_sc, mx_a[h], state[h])
            for h in heads:
                mx_c[h] = scores(g + 2, h, sa_sc)
                state[h] = weights(g + 1, h, sb_sc, mx_b[h], state[h])
            mx_a = mx_c
        return state, mx_a

    state, _ = lax.fori_loop(0, (qi + (NG * GB - 1)) // (NG * GB), past_step, (state, mx_a))
    out_t = jnp.concatenate([state[h][1][0:DHA] / state[h][1][DHA:DHA + 1] for h in heads], axis=0)
    out_ref[...] = out_t.T.astype(BF16)


def _moba(pa, vt):
    bsz, s, _ = pa.shape
    BS = MOBA_BLOCK
    nh = MOBA_STEP_HEADS
    w = nh * MOBA_HEAD_DIM
    nstep = MOBA_HEADS // nh
    nb = s // BS
    gb = MOBA_GROUP
    assert nb % gb == 0 and gb <= SUBLANES and w % LANES == 0
    return pl.pallas_call(
        _moba_kernel,
        out_shape=jax.ShapeDtypeStruct((bsz, s, nstep * w), BF16),
        grid=(bsz, nstep, nb),
        in_specs=[pl.BlockSpec((None, BS, w), lambda b, p, i: (b, i, p)),
                  pl.BlockSpec((None, s, w), lambda b, p, i: (b, 0, nstep + p)),
                  pl.BlockSpec((None, nb // gb, w, gb * BS), lambda b, p, i: (b, 0, p, 0))],
        out_specs=pl.BlockSpec((None, BS, w), lambda b, p, i: (b, i, p)),
        scratch_shapes=[pltpu.VMEM((nb, w), F32), pltpu.VMEM((nh, nb + SUBLANES, BS), F32),
                        pltpu.VMEM((nh, gb, BS, BS), F32), pltpu.VMEM((nh, gb, BS, BS), F32)],
        compiler_params=pltpu.CompilerParams(dimension_semantics=("arbitrary",) * 3,
                                             vmem_limit_bytes=VMEM_LIMIT),
        name="moba",
    )(pa, pa, vt)


def _outproj_kernel(hm_ref, ha_ref, x_ref, gate_ref, shift_ref, scale_ref, g_ref, wo_ref, wr_ref, br_ref,
                    x1_ref, hn_ref, ri_ref):
    dm = hm_ref.shape[1]
    mix = _dot(hm_ref[...], wo_ref[0:dm, :]) + _dot(ha_ref[...], wo_ref[dm:, :])
    x1 = x_ref[...] + gate_ref[...] * mix
    x1_ref[...] = x1
    ms = jnp.mean(x1 * x1, axis=-1, keepdims=True)
    hn = x1 * lax.rsqrt(ms + EPS) * g_ref[...] * (1.0 + scale_ref[...]) + shift_ref[...]
    _store_rows(hn_ref, hn)
    logits = _dot(hn.astype(BF16), wr_ref[...]) + br_ref[...]

    tm = logits.shape[0]
    lane = _iota((tm, LANES), 1)
    lanef = lane.astype(F32)
    big = float(LANES)
    in_groups = lane < N_GROUPS
    gl = jnp.where(in_groups, logits, -jnp.inf)
    gmax = jnp.max(gl, axis=-1, keepdims=True)
    gidx = jnp.min(jnp.where(gl == gmax, lanef, big), axis=-1, keepdims=True)
    g_w = 1.0 / jnp.sum(jnp.where(in_groups, jnp.exp(gl - gmax), 0.0), axis=-1, keepdims=True)
    lo = N_GROUPS + EXPERTS_PER_GROUP * gidx
    el = jnp.where(jnp.logical_and(lanef >= lo, lanef < lo + EXPERTS_PER_GROUP), logits, -jnp.inf)
    e1 = jnp.max(el, axis=-1, keepdims=True)
    i1 = jnp.min(jnp.where(el == e1, lanef, big), axis=-1, keepdims=True)
    el = jnp.where(lanef == i1, -jnp.inf, el)
    e2 = jnp.max(el, axis=-1, keepdims=True)
    i2 = jnp.min(jnp.where(el == e2, lanef, big), axis=-1, keepdims=True)
    t = jnp.exp(e2 - e1)
    w1 = g_w / (1.0 + t)
    w2 = g_w * t / (1.0 + t)
    ri = jnp.where(lane == 0, i1 - N_GROUPS,
                   jnp.where(lane == 1, i2 - N_GROUPS,
                             jnp.where(lane == 2, w1, jnp.where(lane == 3, w2, 0.0))))
    ri_ref[...] = ri


def _outproj(hm, ha, x, mod, g2, wo, wr, br):
    bsz, s, d = x.shape
    tm = min(TOKEN_TILE, s)
    dh = hm.shape[2]
    const = lambda b, i: (0, 0)
    tile = lambda w: pl.BlockSpec((None, tm, w), lambda b, i: (b, i, 0))
    modrow = lambda k: pl.BlockSpec((None, None, 1, d), lambda b, i: (b, k, 0, 0))
    return pl.pallas_call(
        _outproj_kernel,
        out_shape=(jax.ShapeDtypeStruct((bsz, s, d), F32),
                   jax.ShapeDtypeStruct((bsz, s * SUBLANES, LANES), F32),
                   jax.ShapeDtypeStruct((bsz, s, LANES), F32)),
        grid=(bsz, s // tm),
        in_specs=[tile(dh), tile(dh), tile(d), modrow(2), modrow(3), modrow(4),
                  pl.BlockSpec((1, d), const),
                  pl.BlockSpec((2 * dh, d), const),
                  pl.BlockSpec((d, LANES), const),
                  pl.BlockSpec((1, LANES), const)],
        out_specs=(tile(d), pl.BlockSpec((None, tm * SUBLANES, LANES), lambda b, i: (b, i, 0)), tile(LANES)),
        compiler_params=pltpu.CompilerParams(dimension_semantics=("arbitrary", "arbitrary"),
                                             vmem_limit_bytes=VMEM_LIMIT),
        name="outproj",
    )(hm, ha, x, mod, mod, mod, g2, wo, wr, br)


def _rank_kernel(ri_ref, dest_ref, cnt_ref, cnt_sc, run_sc):
    ph = pl.program_id(0)
    i = pl.program_id(1)
    tm = ri_ref.shape[0]
    lane = _iota((tm, LANES), 1).astype(F32)
    ri = ri_ref[...]
    oh0 = lane == ri[:, 0:1]
    oh1 = lane == ri[:, 1:2]
    oh = jnp.where(jnp.logical_or(oh0, oh1), 1.0, 0.0)
    colsum = jnp.sum(oh, axis=0, keepdims=True)

    @pl.when(jnp.logical_and(ph == 0, i == 0))
    def _():
        cnt_sc[...] = jnp.zeros_like(cnt_sc)

    @pl.when(ph == 0)
    def _():
        cnt_sc[...] = cnt_sc[...] + colsum
        dest_ref[...] = jnp.zeros_like(dest_ref)
        cnt_ref[...] = cnt_sc[...]

    @pl.when(jnp.logical_and(ph == 1, i == 0))
    def _():
        padded = jnp.ceil(cnt_sc[...] / ROW_BLOCK) * ROW_BLOCK
        l8 = _iota((SUBLANES, LANES), 1)
        acc = padded
        k = 1
        while k < LANES:
            acc = acc + jnp.where(l8 >= k, pltpu.roll(acc, k, 1), 0.0)
            k *= 2
        run_sc[...] = acc - padded

    @pl.when(ph == 1)
    def _():
        row = _iota((tm, tm), 0)
        col = _iota((tm, tm), 1)
        before = _dot((col < row).astype(BF16), oh.astype(BF16))
        tot = before + run_sc[0:1, :]
        d0 = jnp.sum(jnp.where(oh0, tot, 0.0), axis=-1, keepdims=True)
        d1 = jnp.sum(jnp.where(oh1, tot, 0.0), axis=-1, keepdims=True)
        lane_i = _iota((tm, LANES), 1)
        dest_ref[...] = jnp.where(lane_i == 0, d0, jnp.where(lane_i == 1, d1, 0.0))
        run_sc[...] = run_sc[...] + colsum
        cnt_ref[...] = cnt_sc[...]


def _rank(ri):
    t = ri.shape[0]
    tm = min(TOKEN_TILE, t)
    return pl.pallas_call(
        _rank_kernel,
        out_shape=(jax.ShapeDtypeStruct((t, LANES), F32),
                   jax.ShapeDtypeStruct((SUBLANES, LANES), F32)),
        grid=(2, t // tm),
        in_specs=[pl.BlockSpec((tm, LANES), lambda p, i: (i, 0))],
        out_specs=(pl.BlockSpec((tm, LANES), lambda p, i: (i * p, 0)),
                   pl.BlockSpec((SUBLANES, LANES), lambda p, i: (0, 0))),
        scratch_shapes=[pltpu.VMEM((SUBLANES, LANES), F32), pltpu.VMEM((SUBLANES, LANES), F32)],
        compiler_params=pltpu.CompilerParams(dimension_semantics=("arbitrary", "arbitrary")),
        name="rank",
    )(ri)


def _dispatch_kernel(pe_ref, nu_ref, dest_ref, hn_ref, xpad_ref, zbuf, sem, zsem):
    tm = hn_ref.shape[0] // SUBLANES
    block_rows = ROW_BLOCK * SUBLANES
    nblk = xpad_ref.shape[0] // block_rows

    def zero_block(j):
        rows = pl.ds(pl.multiple_of(j * block_rows, block_rows), block_rows)
        return pltpu.make_async_copy(zbuf, xpad_ref.at[rows, :], zsem)

    @pl.when(pl.program_id(0) == 0)
    def _():
        zbuf[...] = jnp.zeros_like(zbuf)

        def tail(e, n):
            first = jnp.where(e == 0, 0, pe_ref[jnp.maximum(e - 1, 0)])
            nonempty = pe_ref[e] > first

            @pl.when(nonempty)
            def _():
                zero_block(pe_ref[e] - 1).start()
            return n + nonempty.astype(jnp.int32)
        started = lax.fori_loop(0, N_EXPERTS, tail, 0)

        def unused(j, c):
            zero_block(j).start()
            return c
        lax.fori_loop(nu_ref[0], nblk, unused, 0)

        def drain(j, c):
            zero_block(0).wait()
            return c
        lax.fori_loop(0, started + nblk - nu_ref[0], drain, 0)

    def row_copy(t, k):
        d = dest_ref[0, 2 * t + k]
        return pltpu.make_async_copy(_token_tile(hn_ref, t), _token_tile(xpad_ref, d), sem)

    def issue(t, carry):
        row_copy(t, 0).start(priority=0)
        row_copy(t, 1).start(priority=1)
        return carry
    lax.fori_loop(0, tm, issue, 0, unroll=ROW_DMA_UNROLL)

    for _ in range(2):
        pltpu.make_async_copy(hn_ref, xpad_ref.at[pl.ds(0, tm * SUBLANES), :], sem).wait()


def _dispatch(pad_ends, n_used, dest, hn, rows):
    t = hn.shape[0] // SUBLANES
    tm = min(TOKEN_TILE, t)
    return pl.pallas_call(
        _dispatch_kernel,
        out_shape=jax.ShapeDtypeStruct((rows * SUBLANES, LANES), hn.dtype),
        grid_spec=pltpu.PrefetchScalarGridSpec(
            num_scalar_prefetch=2,
            grid=(t // tm,),
            in_specs=[pl.BlockSpec((None, 1, 2 * tm), lambda i, pe, nu: (i, 0, 0), memory_space=pltpu.SMEM),
                      pl.BlockSpec((tm * SUBLANES, LANES), lambda i, pe, nu: (i, 0))],
            out_specs=pl.BlockSpec(memory_space=pl.ANY),
            scratch_shapes=[pltpu.VMEM((ROW_BLOCK * SUBLANES, LANES), hn.dtype),
                            pltpu.SemaphoreType.DMA, pltpu.SemaphoreType.DMA]),
        compiler_params=pltpu.CompilerParams(dimension_semantics=("arbitrary",)),
        name="dispatch",
    )(pad_ends, n_used, dest.reshape(t // tm, 1, 2 * tm), hn)


def _expert_kernel(be_ref, nu_ref, x_ref, wg_ref, wu_ref, wd_ref, y_ref, wgb, wub, wdb):
    i = pl.program_id(0)
    live = i < nu_ref[0]
    new_expert = jnp.logical_or(i == 0, be_ref[i] != be_ref[jnp.maximum(i - 1, 0)])

    @pl.when(jnp.logical_and(live, new_expert))
    def _():
        wgb[...] = wg_ref[...].astype(BF16)
        wub[...] = wu_ref[...].astype(BF16)
        wdb[...] = wd_ref[...].astype(BF16)

    @pl.when(live)
    def _():
        xb = _load_rows(x_ref).astype(BF16)
        g = _dot(xb, wgb[...])
        u = _dot(xb, wub[...])
        _store_rows(y_ref, _dot((g * _sigmoid(g) * u).astype(BF16), wdb[...]))

    @pl.when(jnp.logical_not(live))
    def _():
        y_ref[...] = jnp.zeros_like(y_ref)


def _experts(blk_expert, n_used, xpad, wg, wu, wd):
    rows = xpad.shape[0] // SUBLANES
    d, de = wg.shape[1], wg.shape[2]
    assert d == SUBLANES * LANES
    row_block = (ROW_BLOCK * SUBLANES, LANES)
    nblk = rows // ROW_BLOCK
    live = lambda i, be, nu: jnp.minimum(i, nu[0] - 1)
    return pl.pallas_call(
        _expert_kernel,
        out_shape=jax.ShapeDtypeStruct(xpad.shape, F32),
        grid_spec=pltpu.PrefetchScalarGridSpec(
            num_scalar_prefetch=2,
            grid=(nblk,),
            in_specs=[pl.BlockSpec(row_block, lambda i, be, nu: (live(i, be, nu), 0)),
                      pl.BlockSpec((None, d, de), lambda i, be, nu: (be[live(i, be, nu)], 0, 0)),
                      pl.BlockSpec((None, d, de), lambda i, be, nu: (be[live(i, be, nu)], 0, 0)),
                      pl.BlockSpec((None, de, d), lambda i, be, nu: (be[live(i, be, nu)], 0, 0))],
            out_specs=pl.BlockSpec(row_block, lambda i, be, nu: (i, 0)),
            scratch_shapes=[pltpu.VMEM((d, de), BF16), pltpu.VMEM((d, de), BF16), pltpu.VMEM((de, d), BF16)]),
        compiler_params=pltpu.CompilerParams(dimension_semantics=("arbitrary",),
                                             vmem_limit_bytes=VMEM_LIMIT),
        name="experts",
    )(blk_expert, n_used, xpad, wg, wu, wd)


def _combine_kernel(final_norm, dest_ref, dnext_ref, ypad_ref, x1_ref, ri_ref, gate_ref, gf_ref, out_ref, ybuf, sem):
    tm = x1_ref.shape[0]
    step = pl.program_id(0) * pl.num_programs(1) + pl.program_id(1)
    nsteps = pl.num_programs(0) * pl.num_programs(1)
    slot = step % 2

    def gather(idx_ref, buf):
        def row_copy(t, k):
            d = idx_ref[0, 2 * t + k]
            return pltpu.make_async_copy(_token_tile(ypad_ref, d), _token_tile(ybuf.at[buf, k], t), sem.at[buf])

        def issue(t, carry):
            row_copy(t, 0).start(priority=0)
            row_copy(t, 1).start(priority=1)
            return carry
        lax.fori_loop(0, tm, issue, 0, unroll=ROW_DMA_UNROLL)

    @pl.when(step == 0)
    def _():
        gather(dest_ref, 0)

    @pl.when(step + 1 < nsteps)
    def _():
        gather(dnext_ref, 1 - slot)

    for k in range(2):
        pltpu.make_async_copy(ypad_ref.at[pl.ds(0, tm * SUBLANES), :], ybuf.at[slot, k], sem.at[slot]).wait()

    ri = ri_ref[...]
    ffn = ri[:, 2:3] * _load_rows(ybuf.at[slot, 0]) + ri[:, 3:4] * _load_rows(ybuf.at[slot, 1])
    x2 = x1_ref[...] + gate_ref[...] * ffn
    if final_norm:
        ms = jnp.mean(x2 * x2, axis=-1, keepdims=True)
        x2 = x2 * lax.rsqrt(ms + EPS) * gf_ref[...]
    out_ref[...] = x2


def _combine(dest, ypad, x1, ri, mod, gf, final_norm):
    bsz, s, d = x1.shape
    tm = min(COMBINE_TILE, s)
    nt = s // tm
    dest3 = dest.reshape(bsz * nt, 1, 2 * tm)
    return pl.pallas_call(
        functools.partial(_combine_kernel, final_norm),
        out_shape=jax.ShapeDtypeStruct((bsz, s, d), F32),
        grid=(bsz, nt),
        in_specs=[pl.BlockSpec((None, 1, 2 * tm), lambda b, i: (b * nt + i, 0, 0), memory_space=pltpu.SMEM),
                  pl.BlockSpec((None, 1, 2 * tm), lambda b, i: (jnp.minimum(b * nt + i + 1, bsz * nt - 1), 0, 0),
                               memory_space=pltpu.SMEM),
                  pl.BlockSpec(memory_space=pl.ANY),
                  pl.BlockSpec((None, tm, d), lambda b, i: (b, i, 0)),
                  pl.BlockSpec((None, tm, LANES), lambda b, i: (b, i, 0)),
                  pl.BlockSpec((None, None, 1, d), lambda b, i: (b, 5, 0, 0)),
                  pl.BlockSpec((1, d), lambda b, i: (0, 0))],
        out_specs=pl.BlockSpec((None, tm, d), lambda b, i: (b, i, 0)),
        scratch_shapes=[pltpu.VMEM((2, 2, tm * SUBLANES, LANES), F32), pltpu.SemaphoreType.DMA((2,))],
        compiler_params=pltpu.CompilerParams(dimension_semantics=("arbitrary", "arbitrary"),
                                             vmem_limit_bytes=VMEM_LIMIT),
        name="combine",
    )(dest3, dest3, ypad, x1, ri.reshape(bsz, s, LANES), mod, gf)


def _pad_lanes(a):
    return jnp.pad(a, ((0, 0), (0, LANES - a.shape[1])))


def kernel(x, c, w_ada, b_ada, g_norm1, w_in, w_conv, b_conv, b_gates, g_mlstm_head, w_out, g_norm2,
           w_router_group, b_router_group, w_router_expert, b_router_expert, w_expert_gate, w_expert_up,
           w_expert_down, g_final):
    bsz, s, d = x.shape
    t = bsz * s
    dm = MLSTM_HEADS * MLSTM_HEAD_DIM
    da = MOBA_HEADS * MOBA_HEAD_DIM
    ng = 2 * MLSTM_HEADS
    n_assign = 2 * t
    n_blocks = -(-n_assign // ROW_BLOCK) + N_EXPERTS
    rows = n_blocks * ROW_BLOCK

    for l in range(w_ada.shape[0]):
        mod = _adaln(c, w_ada[l], b_ada[l]).reshape(bsz, 6, 1, d)

        w = w_in[l]
        wm = w[:, :4 * dm].astype(BF16)
        wgates = w[:, 4 * dm:4 * dm + ng]
        wa = w[:, 4 * dm + ng:4 * dm + ng + 2 * da].astype(BF16)
        wvt = w[:, 4 * dm + ng + 2 * da:].T.astype(BF16)
        pm, pa, vt, gc, gt = _inproj(x, mod, g_norm1[l].reshape(1, d), wm, wa, wvt,
                                     _pad_lanes(wgates).astype(BF16), wgates.T.astype(BF16),
                                     _pad_lanes(b_gates[l].reshape(1, ng)), b_gates[l].reshape(ng, 1))
        hm = _mlstm(pm, gc, gt, w_conv[l], b_conv[l], g_mlstm_head[l])
        ha = _moba(pa, vt)

        wr = _pad_lanes(jnp.concatenate([w_router_group[l], w_router_expert[l]], axis=1)).astype(BF16)
        br = _pad_lanes(jnp.concatenate([b_router_group[l], b_router_expert[l]]).reshape(1, -1))
        x1, hn2, ri = _outproj(hm, ha, x, mod, g_norm2[l].reshape(1, d), w_out[l].astype(BF16), wr, br)
        ri = ri.reshape(t, LANES)
        destf, cnt = _rank(ri)
        dest = destf[:, :2].astype(jnp.int32).reshape(n_assign)
        counts = cnt[0, :N_EXPERTS].astype(jnp.int32)
        pad_ends = jnp.cumsum((counts + ROW_BLOCK - 1) // ROW_BLOCK)
        blk_ids = jnp.arange(n_blocks, dtype=jnp.int32)
        blk_expert = jnp.minimum(jnp.sum((pad_ends[None, :] <= blk_ids[:, None]).astype(jnp.int32), axis=1),
                                 N_EXPERTS - 1)
        n_used = pad_ends[-1:].astype(jnp.int32)
        xpad = _dispatch(pad_ends.astype(jnp.int32), n_used, dest, hn2.reshape(t * SUBLANES, LANES), rows)
        ypad = _experts(blk_expert, n_used, xpad, w_expert_gate[l], w_expert_up[l], w_expert_down[l])
        x = _combine(dest, ypad, x1, ri, mod, g_final.reshape(1, d), l == w_ada.shape[0] - 1)
    return x
```

```python
import functools

import jax
import jax.numpy as jnp
from jax import lax
from jax.experimental import pallas as pl
from jax.experimental.pallas import tpu as pltpu

F32 = jnp.float32
BF16 = jnp.bfloat16

MLSTM_HEADS = 4
MLSTM_HEAD_DIM = 128
CONV_WIDTH = 4
MOBA_HEADS = 8
MOBA_HEAD_DIM = 64
MOBA_BLOCK = 256
MOBA_TOPK = 3
N_GROUPS = 4
EXPERTS_PER_GROUP = 8
N_EXPERTS = N_GROUPS * EXPERTS_PER_GROUP
EPS = 1e-6

LANES = 128
SUBLANES = 8
ROW_BLOCK = 512
MLSTM_CHUNK = 256
TOKEN_TILE = 512
COMBINE_TILE = 256
ROW_DMA_UNROLL = 8
NEG = -1e30
LOG2E = 1.4426950408889634
MOBA_GROUP = 2
MOBA_GROUP_KEYS = MOBA_GROUP * MOBA_BLOCK
MOBA_STEP_HEADS = 4
VMEM_LIMIT = 48 * 1024 * 1024


def _dot(a, b):
    return jnp.dot(a, b, preferred_element_type=F32)


def _dot_nt(a, b):
    return lax.dot_general(a, b, (((1,), (1,)), ((), ())), preferred_element_type=F32)


def _dot_tn(a, b):
    return lax.dot_general(a, b, (((0,), (0,)), ((), ())), preferred_element_type=F32)


def _sigmoid(x):
    return 1.0 / (1.0 + jnp.exp(-x))


def _log_sigmoid(x):
    return jnp.minimum(x, 0.0) - jnp.log1p(jnp.exp(-jnp.abs(x)))


def _split3(x):
    hi = x.astype(BF16)
    r1 = x - hi.astype(F32)
    mid = r1.astype(BF16)
    lo = (r1 - mid.astype(F32)).astype(BF16)
    return hi, mid, lo


def _iota(shape, dim):
    return lax.broadcasted_iota(jnp.int32, shape, dim)


def _load_rows(ref):
    n = ref.shape[0] // SUBLANES
    return jnp.concatenate([ref[pl.ds(j, n, stride=SUBLANES), :] for j in range(SUBLANES)], axis=1)


def _store_rows(ref, val):
    n = ref.shape[0] // SUBLANES
    for j in range(SUBLANES):
        ref[pl.ds(j, n, stride=SUBLANES), :] = val[:, j * LANES:(j + 1) * LANES]


def _token_tile(ref, t):
    return ref.at[pl.ds(pl.multiple_of(t * SUBLANES, SUBLANES), SUBLANES), :]


def _adaln_kernel(c_ref, w_ref, b_ref, o_ref):
    c = c_ref[...]
    o_ref[...] = _dot(c * _sigmoid(c), w_ref[...]) + b_ref[...]


def _adaln(c, w, b):
    bsz, d = c.shape
    n = w.shape[1]
    tn = 1024
    return pl.pallas_call(
        _adaln_kernel,
        out_shape=jax.ShapeDtypeStruct((bsz, n), F32),
        grid=(n // tn,),
        in_specs=[pl.BlockSpec((bsz, d), lambda j: (0, 0)),
                  pl.BlockSpec((d, tn), lambda j: (0, j)),
                  pl.BlockSpec((1, tn), lambda j: (0, j))],
        out_specs=pl.BlockSpec((bsz, tn), lambda j: (0, j)),
        name="adaln",
    )(c, w, b.reshape(1, n))


def _inproj_kernel(x_ref, shift_ref, scale_ref, g_ref, wm_ref, wa_ref, wvt_ref, wg_ref, wgt_ref, bg_ref, bgt_ref,
                   pm_ref, pa_ref, vt_ref, gc_ref, gt_ref):
    x = x_ref[...]
    ms = jnp.mean(x * x, axis=-1, keepdims=True)
    y = x * lax.rsqrt(ms + EPS) * g_ref[...]
    hb = (y * (1.0 + scale_ref[...]) + shift_ref[...]).astype(BF16)
    cw = 512
    for j in range(wm_ref.shape[1] // cw):
        pm_ref[:, j * cw:(j + 1) * cw] = _dot(hb, wm_ref[:, j * cw:(j + 1) * cw])
    for j in range(wa_ref.shape[1] // cw):
        pa_ref[:, j * cw:(j + 1) * cw] = _dot(hb, wa_ref[:, j * cw:(j + 1) * cw]).astype(BF16)
    vt = _dot_nt(wvt_ref[...], hb).astype(BF16)
    for j in range(vt_ref.shape[0]):
        vt_ref[j] = vt[:, j * MOBA_GROUP_KEYS:(j + 1) * MOBA_GROUP_KEYS]
    gc_ref[...] = _dot(hb, wg_ref[...]) + bg_ref[...]
    gt_ref[...] = _dot_nt(wgt_ref[...], hb) + bgt_ref[...]


def _inproj(x, mod, g1, wm, wa, wvt, wg, wgt, bg, bgt):
    bsz, s, d = x.shape
    tm = min(TOKEN_TILE, s)
    nm, na, nv = wm.shape[1], wa.shape[1], wvt.shape[0]
    vb = tm // MOBA_GROUP_KEYS
    ng = wgt.shape[0]
    const = lambda b, i: (0, 0)
    return pl.pallas_call(
        _inproj_kernel,
        out_shape=(jax.ShapeDtypeStruct((bsz, s, nm), F32),
                   jax.ShapeDtypeStruct((bsz, s, na), BF16),
                   jax.ShapeDtypeStruct((bsz, s // MOBA_GROUP_KEYS, nv, MOBA_GROUP_KEYS), BF16),
                   jax.ShapeDtypeStruct((bsz, s, LANES), F32),
                   jax.ShapeDtypeStruct((bsz, ng, s), F32)),
        grid=(bsz, s // tm),
        in_specs=[pl.BlockSpec((None, tm, d), lambda b, i: (b, i, 0)),
                  pl.BlockSpec((None, None, 1, d), lambda b, i: (b, 0, 0, 0)),
                  pl.BlockSpec((None, None, 1, d), lambda b, i: (b, 1, 0, 0)),
                  pl.BlockSpec((1, d), const),
                  pl.BlockSpec((d, nm), const),
                  pl.BlockSpec((d, na), const),
                  pl.BlockSpec((nv, d), const),
                  pl.BlockSpec((d, LANES), const),
                  pl.BlockSpec((ng, d), const),
                  pl.BlockSpec((1, LANES), const),
                  pl.BlockSpec((ng, 1), const)],
        out_specs=(pl.BlockSpec((None, tm, nm), lambda b, i: (b, i, 0)),
                   pl.BlockSpec((None, tm, na), lambda b, i: (b, i, 0)),
                   pl.BlockSpec((None, vb, nv, MOBA_GROUP_KEYS), lambda b, i: (b, i, 0, 0)),
                   pl.BlockSpec((None, tm, LANES), lambda b, i: (b, i, 0)),
                   pl.BlockSpec((None, ng, tm), lambda b, i: (b, 0, i))),
        compiler_params=pltpu.CompilerParams(dimension_semantics=("arbitrary", "arbitrary"),
                                             vmem_limit_bytes=VMEM_LIMIT),
        name="inproj",
    )(x, mod, mod, g1, wm, wa, wvt, wg, wgt, bg, bgt)


def _mlstm_kernel(q_ref, k_ref, v_ref, o_ref, gc_ref, gt_ref, wc_ref, bc_ref, gh_ref, out_ref,
                  qbuf, kbuf, c_sc, n_sc, m_sc):
    L = q_ref.shape[0]
    H, DH = MLSTM_HEADS, MLSTM_HEAD_DIM
    HD = H * DH
    ci = pl.program_id(1)

    @pl.when(ci == 0)
    def _():
        qbuf[0:SUBLANES, :] = jnp.zeros((SUBLANES, HD), F32)
        kbuf[0:SUBLANES, :] = jnp.zeros((SUBLANES, HD), F32)
        c_sc[...] = jnp.zeros_like(c_sc)
        n_sc[...] = jnp.zeros_like(n_sc)
        m_sc[...] = jnp.zeros_like(m_sc)

    qbuf[SUBLANES:SUBLANES + L, :] = q_ref[...]
    kbuf[SUBLANES:SUBLANES + L, :] = k_ref[...]
    qc = jnp.broadcast_to(bc_ref[:, 0:HD], (L, HD))
    kc = jnp.broadcast_to(bc_ref[:, HD:2 * HD], (L, HD))
    for j in range(CONV_WIDTH):
        off = SUBLANES - (CONV_WIDTH - 1) + j
        qc = qc + wc_ref[j:j + 1, 0:HD] * qbuf[off:off + L, :]
        kc = kc + wc_ref[j:j + 1, HD:2 * HD] * kbuf[off:off + L, :]
    qbuf[0:SUBLANES, :] = qbuf[L:L + SUBLANES, :]
    kbuf[0:SUBLANES, :] = kbuf[L:L + SUBLANES, :]
    qa = qc * _sigmoid(qc) * (DH ** -0.5)
    ka = kc * _sigmoid(kc)

    gc = gc_ref[...]
    gt = gt_ref[...]
    row = _iota((L, L), 0)
    col = _iota((L, L), 1)
    causal = col <= row
    tril = causal.astype(BF16)
    triu = (row <= col).astype(BF16)
    h3 = _split3(_log_sigmoid(gc))
    cum_c = _dot(tril, h3[0]) + _dot(tril, h3[1]) + _dot(tril, h3[2])
    r3 = _split3(_log_sigmoid(gt))
    cum_r = _dot(r3[0], triu) + _dot(r3[1], triu) + _dot(r3[2], triu)

    for h in range(H):
        sl = slice(h * DH, (h + 1) * DH)
        qf = qa[:, sl]
        qb = qf.astype(BF16)
        kf = ka[:, sl]
        kb = kf.astype(BF16)
        vb = v_ref[:, sl].astype(BF16)
        a_c = cum_c[:, H + h:H + h + 1]
        li_c = gc[:, h:h + 1]
        b_r = gt[h:h + 1, :] - cum_r[H + h:H + h + 1, :]
        cum_last = cum_r[H + h:H + h + 1, L - 1:L]
        m_prev = m_sc[h:h + 1, 0:1]
        c_prev = c_sc[h]
        n_prev = n_sc[h:h + 1, :]

        dm = jnp.where(causal, a_c + b_r, -jnp.inf)
        inter = a_c + m_prev
        m_t = jnp.maximum(inter, jnp.max(dm, axis=-1, keepdims=True))
        w_inter = jnp.exp(inter - m_t)
        sc = _dot_nt(qb, kb) * jnp.exp(dm - m_t)
        num = w_inter * _dot(qb, c_prev.astype(BF16)) + _dot(sc.astype(BF16), vb)
        den = w_inter * jnp.sum(qf * n_prev, axis=-1, keepdims=True) + jnp.sum(sc, axis=-1, keepdims=True)
        ht = num / jnp.maximum(jnp.abs(den), jnp.exp(-m_t))

        g_end_r = cum_last + b_r
        m_new = jnp.maximum(cum_last + m_prev, jnp.max(g_end_r, axis=-1, keepdims=True))
        w_old = jnp.exp(cum_last + m_prev - m_new)
        w_new_c = jnp.exp(cum_last + li_c - a_c - m_new)
        kw = kf * w_new_c
        c_sc[h] = w_old * c_prev + _dot_tn(kw.astype(BF16), vb)
        n_sc[h:h + 1, :] = w_old * n_prev + jnp.sum(kw, axis=0, keepdims=True)
        m_sc[h:h + 1, :] = jnp.broadcast_to(m_new, (1, LANES))

        hn = ht * lax.rsqrt(jnp.mean(ht * ht, axis=-1, keepdims=True) + EPS) * gh_ref[:, sl]
        out_ref[:, sl] = (hn * _sigmoid(o_ref[:, sl])).astype(BF16)


def _mlstm(pm, gc, gt, w_conv, b_conv, g_head):
    bsz, s, _ = pm.shape
    L = min(MLSTM_CHUNK, s)
    H, DH = MLSTM_HEADS, MLSTM_HEAD_DIM
    HD = H * DH
    const = lambda b, i: (0, 0)
    col_block = lambda j: pl.BlockSpec((None, L, HD), lambda b, i: (b, i, j))
    return pl.pallas_call(
        _mlstm_kernel,
        out_shape=jax.ShapeDtypeStruct((bsz, s, HD), BF16),
        grid=(bsz, s // L),
        in_specs=[col_block(0), col_block(1), col_block(2), col_block(3),
                  pl.BlockSpec((None, L, LANES), lambda b, i: (b, i, 0)),
                  pl.BlockSpec((None, 2 * H, L), lambda b, i: (b, 0, i)),
                  pl.BlockSpec((CONV_WIDTH, 2 * HD), const),
                  pl.BlockSpec((1, 2 * HD), const),
                  pl.BlockSpec((1, HD), const)],
        out_specs=pl.BlockSpec((None, L, HD), lambda b, i: (b, i, 0)),
        scratch_shapes=[pltpu.VMEM((L + SUBLANES, HD), F32),
                        pltpu.VMEM((L + SUBLANES, HD), F32),
                        pltpu.VMEM((H, DH, DH), F32),
                        pltpu.VMEM((SUBLANES, DH), F32),
                        pltpu.VMEM((SUBLANES, LANES), F32)],
        compiler_params=pltpu.CompilerParams(dimension_semantics=("arbitrary", "arbitrary"),
                                             vmem_limit_bytes=VMEM_LIMIT),
        name="mlstm",
    )(pm, pm, pm, pm, gc, gt, w_conv, b_conv.reshape(1, 2 * HD), g_head.reshape(1, HD))


def _moba_kernel(q_ref, k_ref, vt_ref, out_ref, km_sc, bias_sc, sa_sc, sb_sc):
    BS = MOBA_BLOCK
    DHA = MOBA_HEAD_DIM
    GB = MOBA_GROUP
    GK = GB * BS
    nb = k_ref.shape[0] // BS
    ngroups = nb // GB
    qi = pl.program_id(2)

    @pl.when(qi == 0)
    def _():
        def mean_body(j, carry):
            kj = k_ref[pl.ds(pl.multiple_of(j * BS, BS), BS), :].astype(F32)
            km_sc[pl.ds(j, 1), :] = jnp.mean(kj, axis=0, keepdims=True)
            return carry
        lax.fori_loop(0, nb, mean_body, 0)

    NH = sa_sc.shape[0]
    heads = range(NH)

    def pair_lanes(h):
        return slice((h // 2) * LANES, (h // 2 + 1) * LANES)

    lane = _iota((BS, LANES), 1)
    first = lane < DHA
    q = q_ref[...].astype(F32) * (DHA ** -0.5 * LOG2E)
    qh = [jnp.where(first == (h % 2 == 0), q[:, pair_lanes(h)], 0.0).astype(BF16) for h in heads]

    ones = jnp.ones((2 * SUBLANES, GK), BF16)

    def values(g, h):
        return jnp.concatenate([vt_ref[g, h * DHA:(h + 1) * DHA, :], ones], axis=0)

    def bias_row(h, g, c):
        return bias_sc[h, pl.ds(GB * g + c, 1), :]

    def scores(g, h, dst):
        gk = jnp.minimum(g, ngroups - 1)
        rows = pl.ds(pl.multiple_of(gk * GK, GK), GK)
        s = _dot_nt(k_ref[rows, pair_lanes(h)], qh[h])
        mx = []
        for c in range(GB):
            sc = s[c * BS:(c + 1) * BS]
            dst[h, c] = sc
            mx.append(jnp.max(sc, axis=0, keepdims=True))
        return mx

    def weights(g, h, src, mx, state):
        m_old, acc = state
        bias = [bias_row(h, g, c) for c in range(GB)]
        mn = m_old
        for c in range(GB):
            mn = jnp.maximum(mn, mx[c] + bias[c])
        p = [jnp.exp2(src[h, c] - (mn - bias[c])).astype(BF16) for c in range(GB)]
        pv = _dot(values(jnp.minimum(g, ngroups - 1), h), jnp.concatenate(p, axis=0))
        return mn, acc * jnp.exp2(m_old - mn) + pv

    kmb = km_sc[...].astype(BF16)
    gate = [_dot_nt(kmb[:, pair_lanes(h)], qh[h]) for h in heads]
    own = pl.ds(pl.multiple_of(qi * BS, BS), BS)
    s_own = [_dot_nt(k_ref[own, pair_lanes(h)], qh[h]) for h in heads]
    mx_a = [scores(0, h, sa_sc) for h in heads]

    blk = _iota((nb, BS), 0)
    blkf = blk.astype(F32)
    for h in heads:
        g = jnp.where(blk < qi, gate[h], -jnp.inf)
        picked = jnp.zeros((nb, BS), jnp.bool_)
        for _ in range(min(MOBA_TOPK, nb - 1)):
            mx = jnp.max(g, axis=0, keepdims=True)
            idx = jnp.min(jnp.where(g == mx, blkf, float(nb)), axis=0, keepdims=True)
            pick = blkf == idx
            picked = jnp.logical_or(picked, pick)
            g = jnp.where(pick, -jnp.inf, g)
        bias_sc[h, 0:nb] = jnp.where(jnp.logical_and(picked, blk < qi), 0.0, NEG)
        bias_sc[h, nb:nb + SUBLANES] = jnp.full((SUBLANES, BS), NEG, F32)

    key_pos = _iota((BS, BS), 0)
    query_pos = _iota((BS, BS), 1)
    causal = key_pos <= query_pos
    own_cols = pl.ds(pl.multiple_of((qi % GB) * BS, BS), BS)
    state = []
    for h in heads:
        s = jnp.where(causal, s_own[h], NEG)
        mn = jnp.max(s, axis=0, keepdims=True)
        p = jnp.exp2(s - mn).astype(BF16)
        v_own = jnp.concatenate([vt_ref[qi // GB, h * DHA:(h + 1) * DHA, own_cols], ones[:, 0:BS]], axis=0)
        state.append((mn, _dot(v_own, p)))

    def past_step(t, carry):
        state, mx_a = carry
        g = 2 * t
        mx_b, mx_c, state = list(mx_a), list(mx_a), list(state)
        for h in heads:
            mx_b[h] = scores(g + 1, h, sb_sc)
            state[h] = weights(g, h, sa_sc, mx_a[h], state[h])
        for h in heads:
            mx_c[h] = scores(g + 2, h, sa_sc)
            state[h] = weights(g + 1, h, sb_sc, mx_b[h], state[h])
        return tuple(state), tuple(mx_c)

    full_steps = qi // (2 * GB)
    state, mx_a = lax.fori_loop(0, full_steps, past_step, (tuple(state), tuple(mx_a)))

    g_tail = 2 * full_steps
    left = qi - 2 * GB * full_steps

    def tail_one(carry):
        state, mx_a = carry
        return tuple(weights(g_tail, h, sa_sc, mx_a[h], state[h]) for h in heads)

    def tail_two(carry):
        state, mx_a = carry
        mx_b, state = list(mx_a), list(state)
        for h in heads:
            mx_b[h] = scores(g_tail + 1, h, sb_sc)
            state[h] = weights(g_tail, h, sa_sc, mx_a[h], state[h])
        return tuple(weights(g_tail + 1, h, sb_sc, mx_b[h], state[h]) for h in heads)

    def tail_any(carry):
        return lax.cond(left > GB, tail_two, tail_one, carry)

    state = lax.cond(left > 0, tail_any, lambda carry: carry[0], (state, mx_a))
    out_t = jnp.concatenate([state[h][1][0:DHA] / state[h][1][DHA:DHA + 1] for h in heads], axis=0)
    out_ref[...] = out_t.T.astype(BF16)


def _moba(pa, vt):
    bsz, s, _ = pa.shape
    BS = MOBA_BLOCK
    nh = MOBA_STEP_HEADS
    w = nh * MOBA_HEAD_DIM
    nstep = MOBA_HEADS // nh
    nb = s // BS
    gb = MOBA_GROUP
    assert nb % gb == 0 and gb <= SUBLANES and w % LANES == 0
    return pl.pallas_call(
        _moba_kernel,
        out_shape=jax.ShapeDtypeStruct((bsz, s, nstep * w), BF16),
        grid=(bsz, nstep, nb),
        in_specs=[pl.BlockSpec((None, BS, w), lambda b, p, i: (b, i, p)),
                  pl.BlockSpec((None, s, w), lambda b, p, i: (b, 0, nstep + p)),
                  pl.BlockSpec((None, nb // gb, w, gb * BS), lambda b, p, i: (b, 0, p, 0))],
        out_specs=pl.BlockSpec((None, BS, w), lambda b, p, i: (b, i, p)),
        scratch_shapes=[pltpu.VMEM((nb, w), F32), pltpu.VMEM((nh, nb + SUBLANES, BS), F32),
                        pltpu.VMEM((nh, gb, BS, BS), F32), pltpu.VMEM((nh, gb, BS, BS), F32)],
        compiler_params=pltpu.CompilerParams(dimension_semantics=("arbitrary",) * 3,
                                             vmem_limit_bytes=VMEM_LIMIT),
        name="moba",
    )(pa, pa, vt)


def _outproj_kernel(hm_ref, ha_ref, x_ref, gate_ref, shift_ref, scale_ref, g_ref, wo_ref, wr_ref, br_ref,
                    x1_ref, hn_ref, ri_ref):
    dm = hm_ref.shape[1]
    mix = _dot(hm_ref[...], wo_ref[0:dm, :]) + _dot(ha_ref[...], wo_ref[dm:, :])
    x1 = x_ref[...] + gate_ref[...] * mix
    x1_ref[...] = x1
    ms = jnp.mean(x1 * x1, axis=-1, keepdims=True)
    hn = x1 * lax.rsqrt(ms + EPS) * g_ref[...] * (1.0 + scale_ref[...]) + shift_ref[...]
    _store_rows(hn_ref, hn)
    logits = _dot(hn.astype(BF16), wr_ref[...]) + br_ref[...]

    tm = logits.shape[0]
    lane = _iota((tm, LANES), 1)
    lanef = lane.astype(F32)
    big = float(LANES)
    in_groups = lane < N_GROUPS
    gl = jnp.where(in_groups, logits, -jnp.inf)
    gmax = jnp.max(gl, axis=-1, keepdims=True)
    gidx = jnp.min(jnp.where(gl == gmax, lanef, big), axis=-1, keepdims=True)
    g_w = 1.0 / jnp.sum(jnp.where(in_groups, jnp.exp(gl - gmax), 0.0), axis=-1, keepdims=True)
    lo = N_GROUPS + EXPERTS_PER_GROUP * gidx
    el = jnp.where(jnp.logical_and(lanef >= lo, lanef < lo + EXPERTS_PER_GROUP), logits, -jnp.inf)
    e1 = jnp.max(el, axis=-1, keepdims=True)
    i1 = jnp.min(jnp.where(el == e1, lanef, big), axis=-1, keepdims=True)
    el = jnp.where(lanef == i1, -jnp.inf, el)
    e2 = jnp.max(el, axis=-1, keepdims=True)
    i2 = jnp.min(jnp.where(el == e2, lanef, big), axis=-1, keepdims=True)
    t = jnp.exp(e2 - e1)
    w1 = g_w / (1.0 + t)
    w2 = g_w * t / (1.0 + t)
    ri = jnp.where(lane == 0, i1 - N_GROUPS,
                   jnp.where(lane == 1, i2 - N_GROUPS,
                             jnp.where(lane == 2, w1, jnp.where(lane == 3, w2, 0.0))))
    ri_ref[...] = ri


def _outproj(hm, ha, x, mod, g2, wo, wr, br):
    bsz, s, d = x.shape
    tm = min(TOKEN_TILE, s)
    dh = hm.shape[2]
    const = lambda b, i: (0, 0)
    tile = lambda w: pl.BlockSpec((None, tm, w), lambda b, i: (b, i, 0))
    modrow = lambda k: pl.BlockSpec((None, None, 1, d), lambda b, i: (b, k, 0, 0))
    return pl.pallas_call(
        _outproj_kernel,
        out_shape=(jax.ShapeDtypeStruct((bsz, s, d), F32),
                   jax.ShapeDtypeStruct((bsz, s * SUBLANES, LANES), F32),
                   jax.ShapeDtypeStruct((bsz, s, LANES), F32)),
        grid=(bsz, s // tm),
        in_specs=[tile(dh), tile(dh), tile(d), modrow(2), modrow(3), modrow(4),
                  pl.BlockSpec((1, d), const),
                  pl.BlockSpec((2 * dh, d), const),
                  pl.BlockSpec((d, LANES), const),
                  pl.BlockSpec((1, LANES), const)],
        out_specs=(tile(d), pl.BlockSpec((None, tm * SUBLANES, LANES), lambda b, i: (b, i, 0)), tile(LANES)),
        compiler_params=pltpu.CompilerParams(dimension_semantics=("arbitrary", "arbitrary"),
                                             vmem_limit_bytes=VMEM_LIMIT),
        name="outproj",
    )(hm, ha, x, mod, mod, mod, g2, wo, wr, br)


def _rank_kernel(ri_ref, dest_ref, cnt_ref, cnt_sc, run_sc):
    ph = pl.program_id(0)
    i = pl.program_id(1)
    tm = ri_ref.shape[0]
    lane = _iota((tm, LANES), 1).astype(F32)
    ri = ri_ref[...]
    oh0 = lane == ri[:, 0:1]
    oh1 = lane == ri[:, 1:2]
    oh = jnp.where(jnp.logical_or(oh0, oh1), 1.0, 0.0)
    colsum = jnp.sum(oh, axis=0, keepdims=True)

    @pl.when(jnp.logical_and(ph == 0, i == 0))
    def _():
        cnt_sc[...] = jnp.zeros_like(cnt_sc)

    @pl.when(ph == 0)
    def _():
        cnt_sc[...] = cnt_sc[...] + colsum
        dest_ref[...] = jnp.zeros_like(dest_ref)
        cnt_ref[...] = cnt_sc[...]

    @pl.when(jnp.logical_and(ph == 1, i == 0))
    def _():
        padded = jnp.ceil(cnt_sc[...] / ROW_BLOCK) * ROW_BLOCK
        l8 = _iota((SUBLANES, LANES), 1)
        acc = padded
        k = 1
        while k < LANES:
            acc = acc + jnp.where(l8 >= k, pltpu.roll(acc, k, 1), 0.0)
            k *= 2
        run_sc[...] = acc - padded

    @pl.when(ph == 1)
    def _():
        row = _iota((tm, tm), 0)
        col = _iota((tm, tm), 1)
        before = _dot((col < row).astype(BF16), oh.astype(BF16))
        tot = before + run_sc[0:1, :]
        d0 = jnp.sum(jnp.where(oh0, tot, 0.0), axis=-1, keepdims=True)
        d1 = jnp.sum(jnp.where(oh1, tot, 0.0), axis=-1, keepdims=True)
        lane_i = _iota((tm, LANES), 1)
        dest_ref[...] = jnp.where(lane_i == 0, d0, jnp.where(lane_i == 1, d1, 0.0))
        run_sc[...] = run_sc[...] + colsum
        cnt_ref[...] = cnt_sc[...]


def _rank(ri):
    t = ri.shape[0]
    tm = min(TOKEN_TILE, t)
    return pl.pallas_call(
        _rank_kernel,
        out_shape=(jax.ShapeDtypeStruct((t, LANES), F32),
                   jax.ShapeDtypeStruct((SUBLANES, LANES), F32)),
        grid=(2, t // tm),
        in_specs=[pl.BlockSpec((tm, LANES), lambda p, i: (i, 0))],
        out_specs=(pl.BlockSpec((tm, LANES), lambda p, i: (i * p, 0)),
                   pl.BlockSpec((SUBLANES, LANES), lambda p, i: (0, 0))),
        scratch_shapes=[pltpu.VMEM((SUBLANES, LANES), F32), pltpu.VMEM((SUBLANES, LANES), F32)],
        compiler_params=pltpu.CompilerParams(dimension_semantics=("arbitrary", "arbitrary")),
        name="rank",
    )(ri)


def _dispatch_kernel(pe_ref, nu_ref, dest_ref, hn_ref, xpad_ref, zbuf, sem, zsem):
    tm = hn_ref.shape[0] // SUBLANES
    block_rows = ROW_BLOCK * SUBLANES
    nblk = xpad_ref.shape[0] // block_rows

    def zero_block(j):
        rows = pl.ds(pl.multiple_of(j * block_rows, block_rows), block_rows)
        return pltpu.make_async_copy(zbuf, xpad_ref.at[rows, :], zsem)

    @pl.when(pl.program_id(0) == 0)
    def _():
        zbuf[...] = jnp.zeros_like(zbuf)

        def tail(e, n):
            first = jnp.where(e == 0, 0, pe_ref[jnp.maximum(e - 1, 0)])
            nonempty = pe_ref[e] > first

            @pl.when(nonempty)
            def _():
                zero_block(pe_ref[e] - 1).start()
            return n + nonempty.astype(jnp.int32)
        started = lax.fori_loop(0, N_EXPERTS, tail, 0)

        def unused(j, c):
            zero_block(j).start()
            return c
        lax.fori_loop(nu_ref[0], nblk, unused, 0)

        def drain(j, c):
            zero_block(0).wait()
            return c
        lax.fori_loop(0, started + nblk - nu_ref[0], drain, 0)

    def row_copy(t, k):
        d = dest_ref[0, 2 * t + k]
        return pltpu.make_async_copy(_token_tile(hn_ref, t), _token_tile(xpad_ref, d), sem)

    def issue(t, carry):
        row_copy(t, 0).start(priority=0)
        row_copy(t, 1).start(priority=1)
        return carry
    lax.fori_loop(0, tm, issue, 0, unroll=ROW_DMA_UNROLL)

    for _ in range(2):
        pltpu.make_async_copy(hn_ref, xpad_ref.at[pl.ds(0, tm * SUBLANES), :], sem).wait()


def _dispatch(pad_ends, n_used, dest, hn, rows):
    t = hn.shape[0] // SUBLANES
    tm = min(TOKEN_TILE, t)
    return pl.pallas_call(
        _dispatch_kernel,
        out_shape=jax.ShapeDtypeStruct((rows * SUBLANES, LANES), hn.dtype),
        grid_spec=pltpu.PrefetchScalarGridSpec(
            num_scalar_prefetch=2,
            grid=(t // tm,),
            in_specs=[pl.BlockSpec((None, 1, 2 * tm), lambda i, pe, nu: (i, 0, 0), memory_space=pltpu.SMEM),
                      pl.BlockSpec((tm * SUBLANES, LANES), lambda i, pe, nu: (i, 0))],
            out_specs=pl.BlockSpec(memory_space=pl.ANY),
            scratch_shapes=[pltpu.VMEM((ROW_BLOCK * SUBLANES, LANES), hn.dtype),
                            pltpu.SemaphoreType.DMA, pltpu.SemaphoreType.DMA]),
        compiler_params=pltpu.CompilerParams(dimension_semantics=("arbitrary",)),
        name="dispatch",
    )(pad_ends, n_used, dest.reshape(t // tm, 1, 2 * tm), hn)


def _expert_kernel(be_ref, nu_ref, x_ref, wg_ref, wu_ref, wd_ref, y_ref, wgb, wub, wdb):
    i = pl.program_id(0)
    live = i < nu_ref[0]
    new_expert = jnp.logical_or(i == 0, be_ref[i] != be_ref[jnp.maximum(i - 1, 0)])

    @pl.when(jnp.logical_and(live, new_expert))
    def _():
        wgb[...] = wg_ref[...].astype(BF16)
        wub[...] = wu_ref[...].astype(BF16)
        wdb[...] = wd_ref[...].astype(BF16)

    @pl.when(live)
    def _():
        xb = _load_rows(x_ref).astype(BF16)
        g = _dot(xb, wgb[...])
        u = _dot(xb, wub[...])
        _store_rows(y_ref, _dot((g * _sigmoid(g) * u).astype(BF16), wdb[...]))

    @pl.when(jnp.logical_not(live))
    def _():
        y_ref[...] = jnp.zeros_like(y_ref)


def _experts(blk_expert, n_used, xpad, wg, wu, wd):
    rows = xpad.shape[0] // SUBLANES
    d, de = wg.shape[1], wg.shape[2]
    assert d == SUBLANES * LANES
    row_block = (ROW_BLOCK * SUBLANES, LANES)
    nblk = rows // ROW_BLOCK
    live = lambda i, be, nu: jnp.minimum(i, nu[0] - 1)
    return pl.pallas_call(
        _expert_kernel,
        out_shape=jax.ShapeDtypeStruct(xpad.shape, F32),
        grid_spec=pltpu.PrefetchScalarGridSpec(
            num_scalar_prefetch=2,
            grid=(nblk,),
            in_specs=[pl.BlockSpec(row_block, lambda i, be, nu: (live(i, be, nu), 0)),
                      pl.BlockSpec((None, d, de), lambda i, be, nu: (be[live(i, be, nu)], 0, 0)),
                      pl.BlockSpec((None, d, de), lambda i, be, nu: (be[live(i, be, nu)], 0, 0)),
                      pl.BlockSpec((None, de, d), lambda i, be, nu: (be[live(i, be, nu)], 0, 0))],
            out_specs=pl.BlockSpec(row_block, lambda i, be, nu: (i, 0)),
            scratch_shapes=[pltpu.VMEM((d, de), BF16), pltpu.VMEM((d, de), BF16), pltpu.VMEM((de, d), BF16)]),
        compiler_params=pltpu.CompilerParams(dimension_semantics=("arbitrary",),
                                             vmem_limit_bytes=VMEM_LIMIT),
        name="experts",
    )(blk_expert, n_used, xpad, wg, wu, wd)


def _combine_kernel(final_norm, dest_ref, dnext_ref, ypad_ref, x1_ref, ri_ref, gate_ref, gf_ref, out_ref, ybuf, sem):
    tm = x1_ref.shape[0]
    step = pl.program_id(0) * pl.num_programs(1) + pl.program_id(1)
    nsteps = pl.num_programs(0) * pl.num_programs(1)
    slot = step % 2

    def gather(idx_ref, buf):
        def row_copy(t, k):
            d = idx_ref[0, 2 * t + k]
            return pltpu.make_async_copy(_token_tile(ypad_ref, d), _token_tile(ybuf.at[buf, k], t), sem.at[buf])

        def issue(t, carry):
            row_copy(t, 0).start(priority=0)
            row_copy(t, 1).start(priority=1)
            return carry
        lax.fori_loop(0, tm, issue, 0, unroll=ROW_DMA_UNROLL)

    @pl.when(step == 0)
    def _():
        gather(dest_ref, 0)

    @pl.when(step + 1 < nsteps)
    def _():
        gather(dnext_ref, 1 - slot)

    for k in range(2):
        pltpu.make_async_copy(ypad_ref.at[pl.ds(0, tm * SUBLANES), :], ybuf.at[slot, k], sem.at[slot]).wait()

    ri = ri_ref[...]
    ffn = ri[:, 2:3] * _load_rows(ybuf.at[slot, 0]) + ri[:, 3:4] * _load_rows(ybuf.at[slot, 1])
    x2 = x1_ref[...] + gate_ref[...] * ffn
    if final_norm:
        ms = jnp.mean(x2 * x2, axis=-1, keepdims=True)
        x2 = x2 * lax.rsqrt(ms + EPS) * gf_ref[...]
    out_ref[...] = x2


def _combine(dest, ypad, x1, ri, mod, gf, final_norm):
    bsz, s, d = x1.shape
    tm = min(COMBINE_TILE, s)
    nt = s // tm
    dest3 = dest.reshape(bsz * nt, 1, 2 * tm)
    return pl.pallas_call(
        functools.partial(_combine_kernel, final_norm),
        out_shape=jax.ShapeDtypeStruct((bsz, s, d), F32),
        grid=(bsz, nt),
        in_specs=[pl.BlockSpec((None, 1, 2 * tm), lambda b, i: (b * nt + i, 0, 0), memory_space=pltpu.SMEM),
                  pl.BlockSpec((None, 1, 2 * tm), lambda b, i: (jnp.minimum(b * nt + i + 1, bsz * nt - 1), 0, 0),
                               memory_space=pltpu.SMEM),
                  pl.BlockSpec(memory_space=pl.ANY),
                  pl.BlockSpec((None, tm, d), lambda b, i: (b, i, 0)),
                  pl.BlockSpec((None, tm, LANES), lambda b, i: (b, i, 0)),
                  pl.BlockSpec((None, None, 1, d), lambda b, i: (b, 5, 0, 0)),
                  pl.BlockSpec((1, d), lambda b, i: (0, 0))],
        out_specs=pl.BlockSpec((None, tm, d), lambda b, i: (b, i, 0)),
        scratch_shapes=[pltpu.VMEM((2, 2, tm * SUBLANES, LANES), F32), pltpu.SemaphoreType.DMA((2,))],
        compiler_params=pltpu.CompilerParams(dimension_semantics=("arbitrary", "arbitrary"),
                                             vmem_limit_bytes=VMEM_LIMIT),
        name="combine",
    )(dest3, dest3, ypad, x1, ri.reshape(bsz, s, LANES), mod, gf)


def _pad_lanes(a):
    return jnp.pad(a, ((0, 0), (0, LANES - a.shape[1])))


def kernel(x, c, w_ada, b_ada, g_norm1, w_in, w_conv, b_conv, b_gates, g_mlstm_head, w_out, g_norm2,
           w_router_group, b_router_group, w_router_expert, b_router_expert, w_expert_gate, w_expert_up,
           w_expert_down, g_final):
    bsz, s, d = x.shape
    t = bsz * s
    dm = MLSTM_HEADS * MLSTM_HEAD_DIM
    da = MOBA_HEADS * MOBA_HEAD_DIM
    ng = 2 * MLSTM_HEADS
    n_assign = 2 * t
    n_blocks = -(-n_assign // ROW_BLOCK) + N_EXPERTS
    rows = n_blocks * ROW_BLOCK

    for l in range(w_ada.shape[0]):
        mod = _adaln(c, w_ada[l], b_ada[l]).reshape(bsz, 6, 1, d)

        w = w_in[l]
        wm = w[:, :4 * dm].astype(BF16)
        wgates = w[:, 4 * dm:4 * dm + ng]
        wa = w[:, 4 * dm + ng:4 * dm + ng + 2 * da].astype(BF16)
        wvt = w[:, 4 * dm + ng + 2 * da:].T.astype(BF16)
        pm, pa, vt, gc, gt = _inproj(x, mod, g_norm1[l].reshape(1, d), wm, wa, wvt,
                                     _pad_lanes(wgates).astype(BF16), wgates.T.astype(BF16),
                                     _pad_lanes(b_gates[l].reshape(1, ng)), b_gates[l].reshape(ng, 1))
        hm = _mlstm(pm, gc, gt, w_conv[l], b_conv[l], g_mlstm_head[l])
        ha = _moba(pa, vt)

        wr = _pad_lanes(jnp.concatenate([w_router_group[l], w_router_expert[l]], axis=1)).astype(BF16)
        br = _pad_lanes(jnp.concatenate([b_router_group[l], b_router_expert[l]]).reshape(1, -1))
        x1, hn2, ri = _outproj(hm, ha, x, mod, g_norm2[l].reshape(1, d), w_out[l].astype(BF16), wr, br)
        ri = ri.reshape(t, LANES)
        destf, cnt = _rank(ri)
        dest = destf[:, :2].astype(jnp.int32).reshape(n_assign)
        counts = cnt[0, :N_EXPERTS].astype(jnp.int32)
        pad_ends = jnp.cumsum((counts + ROW_BLOCK - 1) // ROW_BLOCK)
        blk_ids = jnp.arange(n_blocks, dtype=jnp.int32)
        blk_expert = jnp.minimum(jnp.sum((pad_ends[None, :] <= blk_ids[:, None]).astype(jnp.int32), axis=1),
                                 N_EXPERTS - 1)
        n_used = pad_ends[-1:].astype(jnp.int32)
        xpad = _dispatch(pad_ends.astype(jnp.int32), n_used, dest, hn2.reshape(t * SUBLANES, LANES), rows)
        ypad = _experts(blk_expert, n_used, xpad, w_expert_gate[l], w_expert_up[l], w_expert_down[l])
        x = _combine(dest, ypad, x1, ri, mod, g_final.reshape(1, d), l == w_ada.shape[0] - 1)
    return x
```

```python
import functools

import jax
import jax.numpy as jnp
from jax import lax
from jax.experimental import pallas as pl
from jax.experimental.pallas import tpu as pltpu

F32 = jnp.float32
BF16 = jnp.bfloat16

MLSTM_HEADS = 4
MLSTM_HEAD_DIM = 128
CONV_WIDTH = 4
MOBA_HEADS = 8
MOBA_HEAD_DIM = 64
MOBA_BLOCK = 256
MOBA_TOPK = 3
N_GROUPS = 4
EXPERTS_PER_GROUP = 8
N_EXPERTS = N_GROUPS * EXPERTS_PER_GROUP
EPS = 1e-6

LANES = 128
SUBLANES = 8
ROW_BLOCK = 512
MLSTM_CHUNK = 256
TOKEN_TILE = 512
COMBINE_TILE = 256
ROW_DMA_UNROLL = 8
NEG = -1e30
LOG2E = 1.4426950408889634
MOBA_GROUP = 2
MOBA_GROUP_KEYS = MOBA_GROUP * MOBA_BLOCK
MOBA_STEP_HEADS = 8
VMEM_LIMIT = 48 * 1024 * 1024


def _dot(a, b):
    return jnp.dot(a, b, preferred_element_type=F32)


def _dot_nt(a, b):
    return lax.dot_general(a, b, (((1,), (1,)), ((), ())), preferred_element_type=F32)


def _dot_tn(a, b):
    return lax.dot_general(a, b, (((0,), (0,)), ((), ())), preferred_element_type=F32)


def _sigmoid(x):
    return 1.0 / (1.0 + jnp.exp(-x))


def _log_sigmoid(x):
    return jnp.minimum(x, 0.0) - jnp.log1p(jnp.exp(-jnp.abs(x)))


def _split3(x):
    hi = x.astype(BF16)
    r1 = x - hi.astype(F32)
    mid = r1.astype(BF16)
    lo = (r1 - mid.astype(F32)).astype(BF16)
    return hi, mid, lo


def _iota(shape, dim):
    return lax.broadcasted_iota(jnp.int32, shape, dim)


def _load_rows(ref):
    n = ref.shape[0] // SUBLANES
    return jnp.concatenate([ref[pl.ds(j, n, stride=SUBLANES), :] for j in range(SUBLANES)], axis=1)


def _store_rows(ref, val):
    n = ref.shape[0] // SUBLANES
    for j in range(SUBLANES):
        ref[pl.ds(j, n, stride=SUBLANES), :] = val[:, j * LANES:(j + 1) * LANES]


def _token_tile(ref, t):
    return ref.at[pl.ds(pl.multiple_of(t * SUBLANES, SUBLANES), SUBLANES), :]


def _adaln_kernel(c_ref, w_ref, b_ref, o_ref):
    c = c_ref[...]
    o_ref[...] = _dot(c * _sigmoid(c), w_ref[...]) + b_ref[...]


def _adaln(c, w, b):
    bsz, d = c.shape
    n = w.shape[1]
    tn = 1024
    return pl.pallas_call(
        _adaln_kernel,
        out_shape=jax.ShapeDtypeStruct((bsz, n), F32),
        grid=(n // tn,),
        in_specs=[pl.BlockSpec((bsz, d), lambda j: (0, 0)),
                  pl.BlockSpec((d, tn), lambda j: (0, j)),
                  pl.BlockSpec((1, tn), lambda j: (0, j))],
        out_specs=pl.BlockSpec((bsz, tn), lambda j: (0, j)),
        name="adaln",
    )(c, w, b.reshape(1, n))


def _inproj_kernel(x_ref, shift_ref, scale_ref, g_ref, wm_ref, wa_ref, wvt_ref, wg_ref, wgt_ref, bg_ref, bgt_ref,
                   pm_ref, pa_ref, vt_ref, gc_ref, gt_ref):
    x = x_ref[...]
    ms = jnp.mean(x * x, axis=-1, keepdims=True)
    y = x * lax.rsqrt(ms + EPS) * g_ref[...]
    hb = (y * (1.0 + scale_ref[...]) + shift_ref[...]).astype(BF16)
    cw = 512
    for j in range(wm_ref.shape[1] // cw):
        pm_ref[:, j * cw:(j + 1) * cw] = _dot(hb, wm_ref[:, j * cw:(j + 1) * cw])
    for j in range(wa_ref.shape[1] // cw):
        pa_ref[:, j * cw:(j + 1) * cw] = _dot(hb, wa_ref[:, j * cw:(j + 1) * cw]).astype(BF16)
    vt = _dot_nt(wvt_ref[...], hb).astype(BF16)
    for j in range(vt_ref.shape[0]):
        vt_ref[j] = vt[:, j * MOBA_GROUP_KEYS:(j + 1) * MOBA_GROUP_KEYS]
    gc_ref[...] = _dot(hb, wg_ref[...]) + bg_ref[...]
    gt_ref[...] = _dot_nt(wgt_ref[...], hb) + bgt_ref[...]


def _inproj(x, mod, g1, wm, wa, wvt, wg, wgt, bg, bgt):
    bsz, s, d = x.shape
    tm = min(TOKEN_TILE, s)
    nm, na, nv = wm.shape[1], wa.shape[1], wvt.shape[0]
    vb = tm // MOBA_GROUP_KEYS
    ng = wgt.shape[0]
    const = lambda b, i: (0, 0)
    return pl.pallas_call(
        _inproj_kernel,
        out_shape=(jax.ShapeDtypeStruct((bsz, s, nm), F32),
                   jax.ShapeDtypeStruct((bsz, s, na), BF16),
                   jax.ShapeDtypeStruct((bsz, s // MOBA_GROUP_KEYS, nv, MOBA_GROUP_KEYS), BF16),
                   jax.ShapeDtypeStruct((bsz, s, LANES), F32),
                   jax.ShapeDtypeStruct((bsz, ng, s), F32)),
        grid=(bsz, s // tm),
        in_specs=[pl.BlockSpec((None, tm, d), lambda b, i: (b, i, 0)),
                  pl.BlockSpec((None, None, 1, d), lambda b, i: (b, 0, 0, 0)),
                  pl.BlockSpec((None, None, 1, d), lambda b, i: (b, 1, 0, 0)),
                  pl.BlockSpec((1, d), const),
                  pl.BlockSpec((d, nm), const),
                  pl.BlockSpec((d, na), const),
                  pl.BlockSpec((nv, d), const),
                  pl.BlockSpec((d, LANES), const),
                  pl.BlockSpec((ng, d), const),
                  pl.BlockSpec((1, LANES), const),
                  pl.BlockSpec((ng, 1), const)],
        out_specs=(pl.BlockSpec((None, tm, nm), lambda b, i: (b, i, 0)),
                   pl.BlockSpec((None, tm, na), lambda b, i: (b, i, 0)),
                   pl.BlockSpec((None, vb, nv, MOBA_GROUP_KEYS), lambda b, i: (b, i, 0, 0)),
                   pl.BlockSpec((None, tm, LANES), lambda b, i: (b, i, 0)),
                   pl.BlockSpec((None, ng, tm), lambda b, i: (b, 0, i))),
        compiler_params=pltpu.CompilerParams(dimension_semantics=("arbitrary", "arbitrary"),
                                             vmem_limit_bytes=VMEM_LIMIT),
        name="inproj",
    )(x, mod, mod, g1, wm, wa, wvt, wg, wgt, bg, bgt)


def _mlstm_kernel(q_ref, k_ref, v_ref, o_ref, gc_ref, gt_ref, wc_ref, bc_ref, gh_ref, out_ref,
                  qbuf, kbuf, c_sc, n_sc, m_sc):
    L = q_ref.shape[0]
    H, DH = MLSTM_HEADS, MLSTM_HEAD_DIM
    HD = H * DH
    ci = pl.program_id(1)

    @pl.when(ci == 0)
    def _():
        qbuf[0:SUBLANES, :] = jnp.zeros((SUBLANES, HD), F32)
        kbuf[0:SUBLANES, :] = jnp.zeros((SUBLANES, HD), F32)
        c_sc[...] = jnp.zeros_like(c_sc)
        n_sc[...] = jnp.zeros_like(n_sc)
        m_sc[...] = jnp.zeros_like(m_sc)

    qbuf[SUBLANES:SUBLANES + L, :] = q_ref[...]
    kbuf[SUBLANES:SUBLANES + L, :] = k_ref[...]
    qc = jnp.broadcast_to(bc_ref[:, 0:HD], (L, HD))
    kc = jnp.broadcast_to(bc_ref[:, HD:2 * HD], (L, HD))
    for j in range(CONV_WIDTH):
        off = SUBLANES - (CONV_WIDTH - 1) + j
        qc = qc + wc_ref[j:j + 1, 0:HD] * qbuf[off:off + L, :]
        kc = kc + wc_ref[j:j + 1, HD:2 * HD] * kbuf[off:off + L, :]
    qbuf[0:SUBLANES, :] = qbuf[L:L + SUBLANES, :]
    kbuf[0:SUBLANES, :] = kbuf[L:L + SUBLANES, :]
    qa = qc * _sigmoid(qc) * (DH ** -0.5)
    ka = kc * _sigmoid(kc)

    gc = gc_ref[...]
    gt = gt_ref[...]
    row = _iota((L, L), 0)
    col = _iota((L, L), 1)
    causal = col <= row
    tril = causal.astype(BF16)
    triu = (row <= col).astype(BF16)
    h3 = _split3(_log_sigmoid(gc))
    cum_c = _dot(tril, h3[0]) + _dot(tril, h3[1]) + _dot(tril, h3[2])
    r3 = _split3(_log_sigmoid(gt))
    cum_r = _dot(r3[0], triu) + _dot(r3[1], triu) + _dot(r3[2], triu)

    for h in range(H):
        sl = slice(h * DH, (h + 1) * DH)
        qf = qa[:, sl]
        qb = qf.astype(BF16)
        kf = ka[:, sl]
        kb = kf.astype(BF16)
        vb = v_ref[:, sl].astype(BF16)
        a_c = cum_c[:, H + h:H + h + 1]
        li_c = gc[:, h:h + 1]
        b_r = gt[h:h + 1, :] - cum_r[H + h:H + h + 1, :]
        cum_last = cum_r[H + h:H + h + 1, L - 1:L]
        m_prev = m_sc[h:h + 1, 0:1]
        c_prev = c_sc[h]
        n_prev = n_sc[h:h + 1, :]

        dm = jnp.where(causal, a_c + b_r, -jnp.inf)
        inter = a_c + m_prev
        m_t = jnp.maximum(inter, jnp.max(dm, axis=-1, keepdims=True))
        w_inter = jnp.exp(inter - m_t)
        sc = _dot_nt(qb, kb) * jnp.exp(dm - m_t)
        num = w_inter * _dot(qb, c_prev.astype(BF16)) + _dot(sc.astype(BF16), vb)
        den = w_inter * jnp.sum(qf * n_prev, axis=-1, keepdims=True) + jnp.sum(sc, axis=-1, keepdims=True)
        ht = num / jnp.maximum(jnp.abs(den), jnp.exp(-m_t))

        g_end_r = cum_last + b_r
        m_new = jnp.maximum(cum_last + m_prev, jnp.max(g_end_r, axis=-1, keepdims=True))
        w_old = jnp.exp(cum_last + m_prev - m_new)
        w_new_c = jnp.exp(cum_last + li_c - a_c - m_new)
        kw = kf * w_new_c
        c_sc[h] = w_old * c_prev + _dot_tn(kw.astype(BF16), vb)
        n_sc[h:h + 1, :] = w_old * n_prev + jnp.sum(kw, axis=0, keepdims=True)
        m_sc[h:h + 1, :] = jnp.broadcast_to(m_new, (1, LANES))

        hn = ht * lax.rsqrt(jnp.mean(ht * ht, axis=-1, keepdims=True) + EPS) * gh_ref[:, sl]
        out_ref[:, sl] = (hn * _sigmoid(o_ref[:, sl])).astype(BF16)


def _mlstm(pm, gc, gt, w_conv, b_conv, g_head):
    bsz, s, _ = pm.shape
    L = min(MLSTM_CHUNK, s)
    H, DH = MLSTM_HEADS, MLSTM_HEAD_DIM
    HD = H * DH
    const = lambda b, i: (0, 0)
    col_block = lambda j: pl.BlockSpec((None, L, HD), lambda b, i: (b, i, j))
    return pl.pallas_call(
        _mlstm_kernel,
        out_shape=jax.ShapeDtypeStruct((bsz, s, HD), BF16),
        grid=(bsz, s // L),
        in_specs=[col_block(0), col_block(1), col_block(2), col_block(3),
                  pl.BlockSpec((None, L, LANES), lambda b, i: (b, i, 0)),
                  pl.BlockSpec((None, 2 * H, L), lambda b, i: (b, 0, i)),
                  pl.BlockSpec((CONV_WIDTH, 2 * HD), const),
                  pl.BlockSpec((1, 2 * HD), const),
                  pl.BlockSpec((1, HD), const)],
        out_specs=pl.BlockSpec((None, L, HD), lambda b, i: (b, i, 0)),
        scratch_shapes=[pltpu.VMEM((L + SUBLANES, HD), F32),
                        pltpu.VMEM((L + SUBLANES, HD), F32),
                        pltpu.VMEM((H, DH, DH), F32),
                        pltpu.VMEM((SUBLANES, DH), F32),
                        pltpu.VMEM((SUBLANES, LANES), F32)],
        compiler_params=pltpu.CompilerParams(dimension_semantics=("arbitrary", "arbitrary"),
                                             vmem_limit_bytes=VMEM_LIMIT),
        name="mlstm",
    )(pm, pm, pm, pm, gc, gt, w_conv, b_conv.reshape(1, 2 * HD), g_head.reshape(1, HD))


def _moba_kernel(q_ref, k_ref, vt_ref, out_ref, km_sc, bias_sc, sa_sc, sb_sc):
    BS = MOBA_BLOCK
    DHA = MOBA_HEAD_DIM
    GB = MOBA_GROUP
    GK = GB * BS
    nb = k_ref.shape[0] // BS
    ngroups = nb // GB
    qi = pl.program_id(2)

    @pl.when(qi == 0)
    def _():
        def mean_body(j, carry):
            kj = k_ref[pl.ds(pl.multiple_of(j * BS, BS), BS), :].astype(F32)
            km_sc[pl.ds(j, 1), :] = jnp.mean(kj, axis=0, keepdims=True)
            return carry
        lax.fori_loop(0, nb, mean_body, 0)

    NH = sa_sc.shape[0]
    heads = range(NH)

    def pair_lanes(h):
        return slice((h // 2) * LANES, (h // 2 + 1) * LANES)

    lane = _iota((BS, LANES), 1)
    first = lane < DHA
    q = q_ref[...].astype(F32) * (DHA ** -0.5 * LOG2E)
    qh = [jnp.where(first == (h % 2 == 0), q[:, pair_lanes(h)], 0.0).astype(BF16) for h in heads]

    ones = jnp.ones((2 * SUBLANES, GK), BF16)

    def values(g, h):
        return jnp.concatenate([vt_ref[g, h * DHA:(h + 1) * DHA, :], ones], axis=0)

    def bias_row(h, g, c):
        return bias_sc[h, pl.ds(GB * g + c, 1), :]

    def scores(g, h, dst):
        gk = jnp.minimum(g, ngroups - 1)
        rows = pl.ds(pl.multiple_of(gk * GK, GK), GK)
        s = _dot_nt(k_ref[rows, pair_lanes(h)], qh[h])
        mx = []
        for c in range(GB):
            sc = s[c * BS:(c + 1) * BS]
            dst[h, c] = sc
            mx.append(jnp.max(sc, axis=0, keepdims=True))
        return mx

    def weights(g, h, src, mx, state):
        m_old, acc = state
        bias = [bias_row(h, g, c) for c in range(GB)]
        mn = m_old
        for c in range(GB):
            mn = jnp.maximum(mn, mx[c] + bias[c])
        p = [jnp.exp2(src[h, c] - (mn - bias[c])).astype(BF16) for c in range(GB)]
        pv = _dot(values(jnp.minimum(g, ngroups - 1), h), jnp.concatenate(p, axis=0))
        return mn, acc * jnp.exp2(m_old - mn) + pv

    kmb = km_sc[...].astype(BF16)
    gate = [_dot_nt(kmb[:, pair_lanes(h)], qh[h]) for h in heads]
    own = pl.ds(pl.multiple_of(qi * BS, BS), BS)
    s_own = [_dot_nt(k_ref[own, pair_lanes(h)], qh[h]) for h in heads]
    mx_a = [scores(0, h, sa_sc) for h in heads]

    blk = _iota((nb, BS), 0)
    blkf = blk.astype(F32)
    for h in heads:
        g = jnp.where(blk < qi, gate[h], -jnp.inf)
        picked = jnp.zeros((nb, BS), jnp.bool_)
        for _ in range(min(MOBA_TOPK, nb - 1)):
            mx = jnp.max(g, axis=0, keepdims=True)
            idx = jnp.min(jnp.where(g == mx, blkf, float(nb)), axis=0, keepdims=True)
            pick = blkf == idx
            picked = jnp.logical_or(picked, pick)
            g = jnp.where(pick, -jnp.inf, g)
        bias_sc[h, 0:nb] = jnp.where(jnp.logical_and(picked, blk < qi), 0.0, NEG)
        bias_sc[h, nb:nb + SUBLANES] = jnp.full((SUBLANES, BS), NEG, F32)

    key_pos = _iota((BS, BS), 0)
    query_pos = _iota((BS, BS), 1)
    causal = key_pos <= query_pos
    own_cols = pl.ds(pl.multiple_of((qi % GB) * BS, BS), BS)
    state = []
    for h in heads:
        s = jnp.where(causal, s_own[h], NEG)
        mn = jnp.max(s, axis=0, keepdims=True)
        p = jnp.exp2(s - mn).astype(BF16)
        v_own = jnp.concatenate([vt_ref[qi // GB, h * DHA:(h + 1) * DHA, own_cols], ones[:, 0:BS]], axis=0)
        state.append((mn, _dot(v_own, p)))

    def past_step(t, carry):
        state, mx_a = carry
        g = 2 * t
        mx_b, mx_c, state = list(mx_a), list(mx_a), list(state)
        for h in heads:
            mx_b[h] = scores(g + 1, h, sb_sc)
            state[h] = weights(g, h, sa_sc, mx_a[h], state[h])
        for h in heads:
            mx_c[h] = scores(g + 2, h, sa_sc)
            state[h] = weights(g + 1, h, sb_sc, mx_b[h], state[h])
        return tuple(state), tuple(mx_c)

    full_steps = qi // (2 * GB)
    state, mx_a = lax.fori_loop(0, full_steps, past_step, (tuple(state), tuple(mx_a)))

    g_tail = 2 * full_steps
    left = qi - 2 * GB * full_steps

    def tail_one(carry):
        state, mx_a = carry
        return tuple(weights(g_tail, h, sa_sc, mx_a[h], state[h]) for h in heads)

    def tail_two(carry):
        state, mx_a = carry
        mx_b, state = list(mx_a), list(state)
        for h in heads:
            mx_b[h] = scores(g_tail + 1, h, sb_sc)
            state[h] = weights(g_tail, h, sa_sc, mx_a[h], state[h])
        return tuple(weights(g_tail + 1, h, sb_sc, mx_b[h], state[h]) for h in heads)

    def tail_any(carry):
        return lax.cond(left > GB, tail_two, tail_one, carry)

    state = lax.cond(left > 0, tail_any, lambda carry: carry[0], (state, mx_a))
    out_t = jnp.concatenate([state[h][1][0:DHA] / state[h][1][DHA:DHA + 1] for h in heads], axis=0)
    out_ref[...] = out_t.T.astype(BF16)


def _moba(pa, vt):
    bsz, s, _ = pa.shape
    BS = MOBA_BLOCK
    nh = MOBA_STEP_HEADS
    w = nh * MOBA_HEAD_DIM
    nstep = MOBA_HEADS // nh
    nb = s // BS
    gb = MOBA_GROUP
    assert nb % gb == 0 and gb <= SUBLANES and w % LANES == 0
    return pl.pallas_call(
        _moba_kernel,
        out_shape=jax.ShapeDtypeStruct((bsz, s, nstep * w), BF16),
        grid=(bsz, nstep, nb),
        in_specs=[pl.BlockSpec((None, BS, w), lambda b, p, i: (b, i, p)),
                  pl.BlockSpec((None, s, w), lambda b, p, i: (b, 0, nstep + p)),
                  pl.BlockSpec((None, nb // gb, w, gb * BS), lambda b, p, i: (b, 0, p, 0))],
        out_specs=pl.BlockSpec((None, BS, w), lambda b, p, i: (b, i, p)),
        scratch_shapes=[pltpu.VMEM((nb, w), F32), pltpu.VMEM((nh, nb + SUBLANES, BS), F32),
                        pltpu.VMEM((nh, gb, BS, BS), F32), pltpu.VMEM((nh, gb, BS, BS), F32)],
        compiler_params=pltpu.CompilerParams(dimension_semantics=("arbitrary",) * 3,
                                             vmem_limit_bytes=VMEM_LIMIT),
        name="moba",
    )(pa, pa, vt)


def _outproj_kernel(hm_ref, ha_ref, x_ref, gate_ref, shift_ref, scale_ref, g_ref, wo_ref, wr_ref, br_ref,
                    x1_ref, hn_ref, ri_ref):
    dm = hm_ref.shape[1]
    mix = _dot(hm_ref[...], wo_ref[0:dm, :]) + _dot(ha_ref[...], wo_ref[dm:, :])
    x1 = x_ref[...] + gate_ref[...] * mix
    x1_ref[...] = x1
    ms = jnp.mean(x1 * x1, axis=-1, keepdims=True)
    hn = x1 * lax.rsqrt(ms + EPS) * g_ref[...] * (1.0 + scale_ref[...]) + shift_ref[...]
    _store_rows(hn_ref, hn)
    logits = _dot(hn.astype(BF16), wr_ref[...]) + br_ref[...]

    tm = logits.shape[0]
    lane = _iota((tm, LANES), 1)
    lanef = lane.astype(F32)
    big = float(LANES)
    in_groups = lane < N_GROUPS
    gl = jnp.where(in_groups, logits, -jnp.inf)
    gmax = jnp.max(gl, axis=-1, keepdims=True)
    gidx = jnp.min(jnp.where(gl == gmax, lanef, big), axis=-1, keepdims=True)
    g_w = 1.0 / jnp.sum(jnp.where(in_groups, jnp.exp(gl - gmax), 0.0), axis=-1, keepdims=True)
    lo = N_GROUPS + EXPERTS_PER_GROUP * gidx
    el = jnp.where(jnp.logical_and(lanef >= lo, lanef < lo + EXPERTS_PER_GROUP), logits, -jnp.inf)
    e1 = jnp.max(el, axis=-1, keepdims=True)
    i1 = jnp.min(jnp.where(el == e1, lanef, big), axis=-1, keepdims=True)
    el = jnp.where(lanef == i1, -jnp.inf, el)
    e2 = jnp.max(el, axis=-1, keepdims=True)
    i2 = jnp.min(jnp.where(el == e2, lanef, big), axis=-1, keepdims=True)
    t = jnp.exp(e2 - e1)
    w1 = g_w / (1.0 + t)
    w2 = g_w * t / (1.0 + t)
    ri = jnp.where(lane == 0, i1 - N_GROUPS,
                   jnp.where(lane == 1, i2 - N_GROUPS,
                             jnp.where(lane == 2, w1, jnp.where(lane == 3, w2, 0.0))))
    ri_ref[...] = ri


def _outproj(hm, ha, x, mod, g2, wo, wr, br):
    bsz, s, d = x.shape
    tm = min(TOKEN_TILE, s)
    dh = hm.shape[2]
    const = lambda b, i: (0, 0)
    tile = lambda w: pl.BlockSpec((None, tm, w), lambda b, i: (b, i, 0))
    modrow = lambda k: pl.BlockSpec((None, None, 1, d), lambda b, i: (b, k, 0, 0))
    return pl.pallas_call(
        _outproj_kernel,
        out_shape=(jax.ShapeDtypeStruct((bsz, s, d), F32),
                   jax.ShapeDtypeStruct((bsz, s * SUBLANES, LANES), F32),
                   jax.ShapeDtypeStruct((bsz, s, LANES), F32)),
        grid=(bsz, s // tm),
        in_specs=[tile(dh), tile(dh), tile(d), modrow(2), modrow(3), modrow(4),
                  pl.BlockSpec((1, d), const),
                  pl.BlockSpec((2 * dh, d), const),
                  pl.BlockSpec((d, LANES), const),
                  pl.BlockSpec((1, LANES), const)],
        out_specs=(tile(d), pl.BlockSpec((None, tm * SUBLANES, LANES), lambda b, i: (b, i, 0)), tile(LANES)),
        compiler_params=pltpu.CompilerParams(dimension_semantics=("arbitrary", "arbitrary"),
                                             vmem_limit_bytes=VMEM_LIMIT),
        name="outproj",
    )(hm, ha, x, mod, mod, mod, g2, wo, wr, br)


def _rank_kernel(ri_ref, dest_ref, cnt_ref, cnt_sc, run_sc):
    ph = pl.program_id(0)
    i = pl.program_id(1)
    tm = ri_ref.shape[0]
    lane = _iota((tm, LANES), 1).astype(F32)
    ri = ri_ref[...]
    oh0 = lane == ri[:, 0:1]
    oh1 = lane == ri[:, 1:2]
    oh = jnp.where(jnp.logical_or(oh0, oh1), 1.0, 0.0)
    colsum = jnp.sum(oh, axis=0, keepdims=True)

    @pl.when(jnp.logical_and(ph == 0, i == 0))
    def _():
        cnt_sc[...] = jnp.zeros_like(cnt_sc)

    @pl.when(ph == 0)
    def _():
        cnt_sc[...] = cnt_sc[...] + colsum
        dest_ref[...] = jnp.zeros_like(dest_ref)
        cnt_ref[...] = cnt_sc[...]

    @pl.when(jnp.logical_and(ph == 1, i == 0))
    def _():
        padded = jnp.ceil(cnt_sc[...] / ROW_BLOCK) * ROW_BLOCK
        l8 = _iota((SUBLANES, LANES), 1)
        acc = padded
        k = 1
        while k < LANES:
            acc = acc + jnp.where(l8 >= k, pltpu.roll(acc, k, 1), 0.0)
            k *= 2
        run_sc[...] = acc - padded

    @pl.when(ph == 1)
    def _():
        row = _iota((tm, tm), 0)
        col = _iota((tm, tm), 1)
        before = _dot((col < row).astype(BF16), oh.astype(BF16))
        tot = before + run_sc[0:1, :]
        d0 = jnp.sum(jnp.where(oh0, tot, 0.0), axis=-1, keepdims=True)
        d1 = jnp.sum(jnp.where(oh1, tot, 0.0), axis=-1, keepdims=True)
        lane_i = _iota((tm, LANES), 1)
        dest_ref[...] = jnp.where(lane_i == 0, d0, jnp.where(lane_i == 1, d1, 0.0))
        run_sc[...] = run_sc[...] + colsum
        cnt_ref[...] = cnt_sc[...]


def _rank(ri):
    t = ri.shape[0]
    tm = min(TOKEN_TILE, t)
    return pl.pallas_call(
        _rank_kernel,
        out_shape=(jax.ShapeDtypeStruct((t, LANES), F32),
                   jax.ShapeDtypeStruct((SUBLANES, LANES), F32)),
        grid=(2, t // tm),
        in_specs=[pl.BlockSpec((tm, LANES), lambda p, i: (i, 0))],
        out_specs=(pl.BlockSpec((tm, LANES), lambda p, i: (i * p, 0)),
                   pl.BlockSpec((SUBLANES, LANES), lambda p, i: (0, 0))),
        scratch_shapes=[pltpu.VMEM((SUBLANES, LANES), F32), pltpu.VMEM((SUBLANES, LANES), F32)],
        compiler_params=pltpu.CompilerParams(dimension_semantics=("arbitrary", "arbitrary")),
        name="rank",
    )(ri)


def _dispatch_kernel(pe_ref, nu_ref, dest_ref, hn_ref, xpad_ref, zbuf, sem, zsem):
    tm = hn_ref.shape[0] // SUBLANES
    block_rows = ROW_BLOCK * SUBLANES
    nblk = xpad_ref.shape[0] // block_rows

    def zero_block(j):
        rows = pl.ds(pl.multiple_of(j * block_rows, block_rows), block_rows)
        return pltpu.make_async_copy(zbuf, xpad_ref.at[rows, :], zsem)

    @pl.when(pl.program_id(0) == 0)
    def _():
        zbuf[...] = jnp.zeros_like(zbuf)

        def tail(e, n):
            first = jnp.where(e == 0, 0, pe_ref[jnp.maximum(e - 1, 0)])
            nonempty = pe_ref[e] > first

            @pl.when(nonempty)
            def _():
                zero_block(pe_ref[e] - 1).start()
            return n + nonempty.astype(jnp.int32)
        started = lax.fori_loop(0, N_EXPERTS, tail, 0)

        def unused(j, c):
            zero_block(j).start()
            return c
        lax.fori_loop(nu_ref[0], nblk, unused, 0)

        def drain(j, c):
            zero_block(0).wait()
            return c
        lax.fori_loop(0, started + nblk - nu_ref[0], drain, 0)

    def row_copy(t, k):
        d = dest_ref[0, 2 * t + k]
        return pltpu.make_async_copy(_token_tile(hn_ref, t), _token_tile(xpad_ref, d), sem)

    def issue(t, carry):
        row_copy(t, 0).start(priority=0)
        row_copy(t, 1).start(priority=1)
        return carry
    lax.fori_loop(0, tm, issue, 0, unroll=ROW_DMA_UNROLL)

    for _ in range(2):
        pltpu.make_async_copy(hn_ref, xpad_ref.at[pl.ds(0, tm * SUBLANES), :], sem).wait()


def _dispatch(pad_ends, n_used, dest, hn, rows):
    t = hn.shape[0] // SUBLANES
    tm = min(TOKEN_TILE, t)
    return pl.pallas_call(
        _dispatch_kernel,
        out_shape=jax.ShapeDtypeStruct((rows * SUBLANES, LANES), hn.dtype),
        grid_spec=pltpu.PrefetchScalarGridSpec(
            num_scalar_prefetch=2,
            grid=(t // tm,),
            in_specs=[pl.BlockSpec((None, 1, 2 * tm), lambda i, pe, nu: (i, 0, 0), memory_space=pltpu.SMEM),
                      pl.BlockSpec((tm * SUBLANES, LANES), lambda i, pe, nu: (i, 0))],
            out_specs=pl.BlockSpec(memory_space=pl.ANY),
            scratch_shapes=[pltpu.VMEM((ROW_BLOCK * SUBLANES, LANES), hn.dtype),
                            pltpu.SemaphoreType.DMA, pltpu.SemaphoreType.DMA]),
        compiler_params=pltpu.CompilerParams(dimension_semantics=("arbitrary",)),
        name="dispatch",
    )(pad_ends, n_used, dest.reshape(t // tm, 1, 2 * tm), hn)


def _expert_kernel(be_ref, nu_ref, x_ref, wg_ref, wu_ref, wd_ref, y_ref, wgb, wub, wdb):
    i = pl.program_id(0)
    live = i < nu_ref[0]
    new_expert = jnp.logical_or(i == 0, be_ref[i] != be_ref[jnp.maximum(i - 1, 0)])

    @pl.when(jnp.logical_and(live, new_expert))
    def _():
        wgb[...] = wg_ref[...].astype(BF16)
        wub[...] = wu_ref[...].astype(BF16)
        wdb[...] = wd_ref[...].astype(BF16)

    @pl.when(live)
    def _():
        xb = _load_rows(x_ref).astype(BF16)
        g = _dot(xb, wgb[...])
        u = _dot(xb, wub[...])
        _store_rows(y_ref, _dot((g * _sigmoid(g) * u).astype(BF16), wdb[...]))

    @pl.when(jnp.logical_not(live))
    def _():
        y_ref[...] = jnp.zeros_like(y_ref)


def _experts(blk_expert, n_used, xpad, wg, wu, wd):
    rows = xpad.shape[0] // SUBLANES
    d, de = wg.shape[1], wg.shape[2]
    assert d == SUBLANES * LANES
    row_block = (ROW_BLOCK * SUBLANES, LANES)
    nblk = rows // ROW_BLOCK
    live = lambda i, be, nu: jnp.minimum(i, nu[0] - 1)
    return pl.pallas_call(
        _expert_kernel,
        out_shape=jax.ShapeDtypeStruct(xpad.shape, F32),
        grid_spec=pltpu.PrefetchScalarGridSpec(
            num_scalar_prefetch=2,
            grid=(nblk,),
            in_specs=[pl.BlockSpec(row_block, lambda i, be, nu: (live(i, be, nu), 0)),
                      pl.BlockSpec((None, d, de), lambda i, be, nu: (be[live(i, be, nu)], 0, 0)),
                      pl.BlockSpec((None, d, de), lambda i, be, nu: (be[live(i, be, nu)], 0, 0)),
                      pl.BlockSpec((None, de, d), lambda i, be, nu: (be[live(i, be, nu)], 0, 0))],
            out_specs=pl.BlockSpec(row_block, lambda i, be, nu: (i, 0)),
            scratch_shapes=[pltpu.VMEM((d, de), BF16), pltpu.VMEM((d, de), BF16), pltpu.VMEM((de, d), BF16)]),
        compiler_params=pltpu.CompilerParams(dimension_semantics=("arbitrary",),
                                             vmem_limit_bytes=VMEM_LIMIT),
        name="experts",
    )(blk_expert, n_used, xpad, wg, wu, wd)


def _combine_kernel(final_norm, dest_ref, dnext_ref, ypad_ref, x1_ref, ri_ref, gate_ref, gf_ref, out_ref, ybuf, sem):
    tm = x1_ref.shape[0]
    step = pl.program_id(0) * pl.num_programs(1) + pl.program_id(1)
    nsteps = pl.num_programs(0) * pl.num_programs(1)
    slot = step % 2

    def gather(idx_ref, buf):
        def row_copy(t, k):
            d = idx_ref[0, 2 * t + k]
            return pltpu.make_async_copy(_token_tile(ypad_ref, d), _token_tile(ybuf.at[buf, k], t), sem.at[buf])

        def issue(t, carry):
            row_copy(t, 0).start(priority=0)
            row_copy(t, 1).start(priority=1)
            return carry
        lax.fori_loop(0, tm, issue, 0, unroll=ROW_DMA_UNROLL)

    @pl.when(step == 0)
    def _():
        gather(dest_ref, 0)

    @pl.when(step + 1 < nsteps)
    def _():
        gather(dnext_ref, 1 - slot)

    for k in range(2):
        pltpu.make_async_copy(ypad_ref.at[pl.ds(0, tm * SUBLANES), :], ybuf.at[slot, k], sem.at[slot]).wait()

    ri = ri_ref[...]
    ffn = ri[:, 2:3] * _load_rows(ybuf.at[slot, 0]) + ri[:, 3:4] * _load_rows(ybuf.at[slot, 1])
    x2 = x1_ref[...] + gate_ref[...] * ffn
    if final_norm:
        ms = jnp.mean(x2 * x2, axis=-1, keepdims=True)
        x2 = x2 * lax.rsqrt(ms + EPS) * gf_ref[...]
    out_ref[...] = x2


def _combine(dest, ypad, x1, ri, mod, gf, final_norm):
    bsz, s, d = x1.shape
    tm = min(COMBINE_TILE, s)
    nt = s // tm
    dest3 = dest.reshape(bsz * nt, 1, 2 * tm)
    return pl.pallas_call(
        functools.partial(_combine_kernel, final_norm),
        out_shape=jax.ShapeDtypeStruct((bsz, s, d), F32),
        grid=(bsz, nt),
        in_specs=[pl.BlockSpec((None, 1, 2 * tm), lambda b, i: (b * nt + i, 0, 0), memory_space=pltpu.SMEM),
                  pl.BlockSpec((None, 1, 2 * tm), lambda b, i: (jnp.minimum(b * nt + i + 1, bsz * nt - 1), 0, 0),
                               memory_space=pltpu.SMEM),
                  pl.BlockSpec(memory_space=pl.ANY),
                  pl.BlockSpec((None, tm, d), lambda b, i: (b, i, 0)),
                  pl.BlockSpec((None, tm, LANES), lambda b, i: (b, i, 0)),
                  pl.BlockSpec((None, None, 1, d), lambda b, i: (b, 5, 0, 0)),
                  pl.BlockSpec((1, d), lambda b, i: (0, 0))],
        out_specs=pl.BlockSpec((None, tm, d), lambda b, i: (b, i, 0)),
        scratch_shapes=[pltpu.VMEM((2, 2, tm * SUBLANES, LANES), F32), pltpu.SemaphoreType.DMA((2,))],
        compiler_params=pltpu.CompilerParams(dimension_semantics=("arbitrary", "arbitrary"),
                                             vmem_limit_bytes=VMEM_LIMIT),
        name="combine",
    )(dest3, dest3, ypad, x1, ri.reshape(bsz, s, LANES), mod, gf)


def _pad_lanes(a):
    return jnp.pad(a, ((0, 0), (0, LANES - a.shape[1])))


def kernel(x, c, w_ada, b_ada, g_norm1, w_in, w_conv, b_conv, b_gates, g_mlstm_head, w_out, g_norm2,
           w_router_group, b_router_group, w_router_expert, b_router_expert, w_expert_gate, w_expert_up,
           w_expert_down, g_final):
    bsz, s, d = x.shape
    t = bsz * s
    dm = MLSTM_HEADS * MLSTM_HEAD_DIM
    da = MOBA_HEADS * MOBA_HEAD_DIM
    ng = 2 * MLSTM_HEADS
    n_assign = 2 * t
    n_blocks = -(-n_assign // ROW_BLOCK) + N_EXPERTS
    rows = n_blocks * ROW_BLOCK

    for l in range(w_ada.shape[0]):
        mod = _adaln(c, w_ada[l], b_ada[l]).reshape(bsz, 6, 1, d)

        w = w_in[l]
        wm = w[:, :4 * dm].astype(BF16)
        wgates = w[:, 4 * dm:4 * dm + ng]
        wa = w[:, 4 * dm + ng:4 * dm + ng + 2 * da].astype(BF16)
        wvt = w[:, 4 * dm + ng + 2 * da:].T.astype(BF16)
        pm, pa, vt, gc, gt = _inproj(x, mod, g_norm1[l].reshape(1, d), wm, wa, wvt,
                                     _pad_lanes(wgates).astype(BF16), wgates.T.astype(BF16),
                                     _pad_lanes(b_gates[l].reshape(1, ng)), b_gates[l].reshape(ng, 1))
        hm = _mlstm(pm, gc, gt, w_conv[l], b_conv[l], g_mlstm_head[l])
        ha = _moba(pa, vt)

        wr = _pad_lanes(jnp.concatenate([w_router_group[l], w_router_expert[l]], axis=1)).astype(BF16)
        br = _pad_lanes(jnp.concatenate([b_router_group[l], b_router_expert[l]]).reshape(1, -1))
        x1, hn2, ri = _outproj(hm, ha, x, mod, g_norm2[l].reshape(1, d), w_out[l].astype(BF16), wr, br)
        ri = ri.reshape(t, LANES)
        destf, cnt = _rank(ri)
        dest = destf[:, :2].astype(jnp.int32).reshape(n_assign)
        counts = cnt[0, :N_EXPERTS].astype(jnp.int32)
        pad_ends = jnp.cumsum((counts + ROW_BLOCK - 1) // ROW_BLOCK)
        blk_ids = jnp.arange(n_blocks, dtype=jnp.int32)
        blk_expert = jnp.minimum(jnp.sum((pad_ends[None, :] <= blk_ids[:, None]).astype(jnp.int32), axis=1),
                                 N_EXPERTS - 1)
        n_used = pad_ends[-1:].astype(jnp.int32)
        xpad = _dispatch(pad_ends.astype(jnp.int32), n_used, dest, hn2.reshape(t * SUBLANES, LANES), rows)
        ypad = _experts(blk_expert, n_used, xpad, w_expert_gate[l], w_expert_up[l], w_expert_down[l])
        x = _combine(dest, ypad, x1, ri, mod, g_final.reshape(1, d), l == w_ada.shape[0] - 1)
    return x
```

```python
import functools

import jax
import jax.numpy as jnp
from jax import lax
from jax.experimental import pallas as pl
from jax.experimental.pallas import tpu as pltpu

F32 = jnp.float32
BF16 = jnp.bfloat16

MLSTM_HEADS = 4
MLSTM_HEAD_DIM = 128
CONV_WIDTH = 4
MOBA_HEADS = 8
MOBA_HEAD_DIM = 64
MOBA_BLOCK = 256
MOBA_TOPK = 3
N_GROUPS = 4
EXPERTS_PER_GROUP = 8
N_EXPERTS = N_GROUPS * EXPERTS_PER_GROUP
EPS = 1e-6

LANES = 128
SUBLANES = 8
ROW_BLOCK = 512
MLSTM_CHUNK = 256
TOKEN_TILE = 512
COMBINE_TILE = 256
ROW_DMA_UNROLL = 8
NEG = -1e30
LOG2E = 1.4426950408889634
MOBA_GROUP = 2
MOBA_GROUP_KEYS = MOBA_GROUP * MOBA_BLOCK
MOBA_STEP_HEADS = 8
VMEM_LIMIT = 48 * 1024 * 1024


def _dot(a, b):
    return jnp.dot(a, b, preferred_element_type=F32)


def _dot_nt(a, b):
    return lax.dot_general(a, b, (((1,), (1,)), ((), ())), preferred_element_type=F32)


def _dot_tn(a, b):
    return lax.dot_general(a, b, (((0,), (0,)), ((), ())), preferred_element_type=F32)


def _sigmoid(x):
    return 1.0 / (1.0 + jnp.exp(-x))


def _log_sigmoid(x):
    return jnp.minimum(x, 0.0) - jnp.log1p(jnp.exp(-jnp.abs(x)))


def _split3(x):
    hi = x.astype(BF16)
    r1 = x - hi.astype(F32)
    mid = r1.astype(BF16)
    lo = (r1 - mid.astype(F32)).astype(BF16)
    return hi, mid, lo


def _iota(shape, dim):
    return lax.broadcasted_iota(jnp.int32, shape, dim)


def _load_rows(ref):
    n = ref.shape[0] // SUBLANES
    return jnp.concatenate([ref[pl.ds(j, n, stride=SUBLANES), :] for j in range(SUBLANES)], axis=1)


def _store_rows(ref, val):
    n = ref.shape[0] // SUBLANES
    for j in range(SUBLANES):
        ref[pl.ds(j, n, stride=SUBLANES), :] = val[:, j * LANES:(j + 1) * LANES]


def _token_tile(ref, t):
    return ref.at[pl.ds(pl.multiple_of(t * SUBLANES, SUBLANES), SUBLANES), :]


def _adaln_kernel(c_ref, w_ref, b_ref, o_ref):
    c = c_ref[...]
    o_ref[...] = _dot(c * _sigmoid(c), w_ref[...]) + b_ref[...]


def _adaln(c, w, b):
    bsz, d = c.shape
    n = w.shape[1]
    tn = 1024
    return pl.pallas_call(
        _adaln_kernel,
        out_shape=jax.ShapeDtypeStruct((bsz, n), F32),
        grid=(n // tn,),
        in_specs=[pl.BlockSpec((bsz, d), lambda j: (0, 0)),
                  pl.BlockSpec((d, tn), lambda j: (0, j)),
                  pl.BlockSpec((1, tn), lambda j: (0, j))],
        out_specs=pl.BlockSpec((bsz, tn), lambda j: (0, j)),
        name="adaln",
    )(c, w, b.reshape(1, n))


def _inproj_kernel(x_ref, shift_ref, scale_ref, g_ref, wm_ref, wa_ref, wvt_ref, wg_ref, wgt_ref, bg_ref, bgt_ref,
                   pm_ref, pa_ref, vt_ref, gc_ref, gt_ref):
    x = x_ref[...]
    ms = jnp.mean(x * x, axis=-1, keepdims=True)
    y = x * lax.rsqrt(ms + EPS) * g_ref[...]
    hb = (y * (1.0 + scale_ref[...]) + shift_ref[...]).astype(BF16)
    cw = 512
    for j in range(wm_ref.shape[1] // cw):
        pm_ref[:, j * cw:(j + 1) * cw] = _dot(hb, wm_ref[:, j * cw:(j + 1) * cw])
    for j in range(wa_ref.shape[1] // cw):
        pa_ref[:, j * cw:(j + 1) * cw] = _dot(hb, wa_ref[:, j * cw:(j + 1) * cw]).astype(BF16)
    vt = _dot_nt(wvt_ref[...], hb).astype(BF16)
    for j in range(vt_ref.shape[0]):
        vt_ref[j] = vt[:, j * MOBA_GROUP_KEYS:(j + 1) * MOBA_GROUP_KEYS]
    gc_ref[...] = _dot(hb, wg_ref[...]) + bg_ref[...]
    gt_ref[...] = _dot_nt(wgt_ref[...], hb) + bgt_ref[...]


def _inproj(x, mod, g1, wm, wa, wvt, wg, wgt, bg, bgt):
    bsz, s, d = x.shape
    tm = min(TOKEN_TILE, s)
    nm, na, nv = wm.shape[1], wa.shape[1], wvt.shape[0]
    vb = tm // MOBA_GROUP_KEYS
    ng = wgt.shape[0]
    const = lambda b, i: (0, 0)
    return pl.pallas_call(
        _inproj_kernel,
        out_shape=(jax.ShapeDtypeStruct((bsz, s, nm), F32),
                   jax.ShapeDtypeStruct((bsz, s, na), BF16),
                   jax.ShapeDtypeStruct((bsz, s // MOBA_GROUP_KEYS, nv, MOBA_GROUP_KEYS), BF16),
                   jax.ShapeDtypeStruct((bsz, s, LANES), F32),
                   jax.ShapeDtypeStruct((bsz, ng, s), F32)),
        grid=(bsz, s // tm),
        in_specs=[pl.BlockSpec((None, tm, d), lambda b, i: (b, i, 0)),
                  pl.BlockSpec((None, None, 1, d), lambda b, i: (b, 0, 0, 0)),
                  pl.BlockSpec((None, None, 1, d), lambda b, i: (b, 1, 0, 0)),
                  pl.BlockSpec((1, d), const),
                  pl.BlockSpec((d, nm), const),
                  pl.BlockSpec((d, na), const),
                  pl.BlockSpec((nv, d), const),
                  pl.BlockSpec((d, LANES), const),
                  pl.BlockSpec((ng, d), const),
                  pl.BlockSpec((1, LANES), const),
                  pl.BlockSpec((ng, 1), const)],
        out_specs=(pl.BlockSpec((None, tm, nm), lambda b, i: (b, i, 0)),
                   pl.BlockSpec((None, tm, na), lambda b, i: (b, i, 0)),
                   pl.BlockSpec((None, vb, nv, MOBA_GROUP_KEYS), lambda b, i: (b, i, 0, 0)),
                   pl.BlockSpec((None, tm, LANES), lambda b, i: (b, i, 0)),
                   pl.BlockSpec((None, ng, tm), lambda b, i: (b, 0, i))),
        compiler_params=pltpu.CompilerParams(dimension_semantics=("arbitrary", "arbitrary"),
                                             vmem_limit_bytes=VMEM_LIMIT),
        name="inproj",
    )(x, mod, mod, g1, wm, wa, wvt, wg, wgt, bg, bgt)


def _mlstm_kernel(q_ref, k_ref, v_ref, o_ref, gc_ref, gt_ref, wc_ref, bc_ref, gh_ref, out_ref,
                  qbuf, kbuf, c_sc, n_sc, m_sc):
    L = q_ref.shape[0]
    H, DH = MLSTM_HEADS, MLSTM_HEAD_DIM
    HD = H * DH
    ci = pl.program_id(1)

    @pl.when(ci == 0)
    def _():
        qbuf[0:SUBLANES, :] = jnp.zeros((SUBLANES, HD), F32)
        kbuf[0:SUBLANES, :] = jnp.zeros((SUBLANES, HD), F32)
        c_sc[...] = jnp.zeros_like(c_sc)
        n_sc[...] = jnp.zeros_like(n_sc)
        m_sc[...] = jnp.zeros_like(m_sc)

    qbuf[SUBLANES:SUBLANES + L, :] = q_ref[...]
    kbuf[SUBLANES:SUBLANES + L, :] = k_ref[...]
    qc = jnp.broadcast_to(bc_ref[:, 0:HD], (L, HD))
    kc = jnp.broadcast_to(bc_ref[:, HD:2 * HD], (L, HD))
    for j in range(CONV_WIDTH):
        off = SUBLANES - (CONV_WIDTH - 1) + j
        qc = qc + wc_ref[j:j + 1, 0:HD] * qbuf[off:off + L, :]
        kc = kc + wc_ref[j:j + 1, HD:2 * HD] * kbuf[off:off + L, :]
    qbuf[0:SUBLANES, :] = qbuf[L:L + SUBLANES, :]
    kbuf[0:SUBLANES, :] = kbuf[L:L + SUBLANES, :]
    qa = qc * _sigmoid(qc) * (DH ** -0.5)
    ka = kc * _sigmoid(kc)

    gc = gc_ref[...]
    gt = gt_ref[...]
    row = _iota((L, L), 0)
    col = _iota((L, L), 1)
    causal = col <= row
    tril = causal.astype(BF16)
    triu = (row <= col).astype(BF16)
    h3 = _split3(_log_sigmoid(gc))
    cum_c = _dot(tril, h3[0]) + _dot(tril, h3[1]) + _dot(tril, h3[2])
    r3 = _split3(_log_sigmoid(gt))
    cum_r = _dot(r3[0], triu) + _dot(r3[1], triu) + _dot(r3[2], triu)

    for h in range(H):
        sl = slice(h * DH, (h + 1) * DH)
        qf = qa[:, sl]
        qb = qf.astype(BF16)
        kf = ka[:, sl]
        kb = kf.astype(BF16)
        vb = v_ref[:, sl].astype(BF16)
        a_c = cum_c[:, H + h:H + h + 1]
        li_c = gc[:, h:h + 1]
        b_r = gt[h:h + 1, :] - cum_r[H + h:H + h + 1, :]
        cum_last = cum_r[H + h:H + h + 1, L - 1:L]
        m_prev = m_sc[h:h + 1, 0:1]
        c_prev = c_sc[h]
        n_prev = n_sc[h:h + 1, :]

        dm = jnp.where(causal, a_c + b_r, -jnp.inf)
        inter = a_c + m_prev
        m_t = jnp.maximum(inter, jnp.max(dm, axis=-1, keepdims=True))
        w_inter = jnp.exp(inter - m_t)
        sc = _dot_nt(qb, kb) * jnp.exp(dm - m_t)
        num = w_inter * _dot(qb, c_prev.astype(BF16)) + _dot(sc.astype(BF16), vb)
        den = w_inter * jnp.sum(qf * n_prev, axis=-1, keepdims=True) + jnp.sum(sc, axis=-1, keepdims=True)
        ht = num / jnp.maximum(jnp.abs(den), jnp.exp(-m_t))

        g_end_r = cum_last + b_r
        m_new = jnp.maximum(cum_last + m_prev, jnp.max(g_end_r, axis=-1, keepdims=True))
        w_old = jnp.exp(cum_last + m_prev - m_new)
        w_new_c = jnp.exp(cum_last + li_c - a_c - m_new)
        kw = kf * w_new_c
        c_sc[h] = w_old * c_prev + _dot_tn(kw.astype(BF16), vb)
        n_sc[h:h + 1, :] = w_old * n_prev + jnp.sum(kw, axis=0, keepdims=True)
        m_sc[h:h + 1, :] = jnp.broadcast_to(m_new, (1, LANES))

        hn = ht * lax.rsqrt(jnp.mean(ht * ht, axis=-1, keepdims=True) + EPS) * gh_ref[:, sl]
        out_ref[:, sl] = (hn * _sigmoid(o_ref[:, sl])).astype(BF16)


def _mlstm(pm, gc, gt, w_conv, b_conv, g_head):
    bsz, s, _ = pm.shape
    L = min(MLSTM_CHUNK, s)
    H, DH = MLSTM_HEADS, MLSTM_HEAD_DIM
    HD = H * DH
    const = lambda b, i: (0, 0)
    col_block = lambda j: pl.BlockSpec((None, L, HD), lambda b, i: (b, i, j))
    return pl.pallas_call(
        _mlstm_kernel,
        out_shape=jax.ShapeDtypeStruct((bsz, s, HD), BF16),
        grid=(bsz, s // L),
        in_specs=[col_block(0), col_block(1), col_block(2), col_block(3),
                  pl.BlockSpec((None, L, LANES), lambda b, i: (b, i, 0)),
                  pl.BlockSpec((None, 2 * H, L), lambda b, i: (b, 0, i)),
                  pl.BlockSpec((CONV_WIDTH, 2 * HD), const),
                  pl.BlockSpec((1, 2 * HD), const),
                  pl.BlockSpec((1, HD), const)],
        out_specs=pl.BlockSpec((None, L, HD), lambda b, i: (b, i, 0)),
        scratch_shapes=[pltpu.VMEM((L + SUBLANES, HD), F32),
                        pltpu.VMEM((L + SUBLANES, HD), F32),
                        pltpu.VMEM((H, DH, DH), F32),
                        pltpu.VMEM((SUBLANES, DH), F32),
                        pltpu.VMEM((SUBLANES, LANES), F32)],
        compiler_params=pltpu.CompilerParams(dimension_semantics=("arbitrary", "arbitrary"),
                                             vmem_limit_bytes=VMEM_LIMIT),
        name="mlstm",
    )(pm, pm, pm, pm, gc, gt, w_conv, b_conv.reshape(1, 2 * HD), g_head.reshape(1, HD))


def _moba_kernel(q_ref, k_ref, vt_ref, out_ref, km_sc, bias_sc, sa_sc, sb_sc):
    BS = MOBA_BLOCK
    DHA = MOBA_HEAD_DIM
    GB = MOBA_GROUP
    GK = GB * BS
    nb = k_ref.shape[0] // BS
    ngroups = nb // GB
    qi = pl.program_id(2)

    @pl.when(qi == 0)
    def _():
        def mean_body(j, carry):
            kj = k_ref[pl.ds(pl.multiple_of(j * BS, BS), BS), :].astype(F32)
            km_sc[pl.ds(j, 1), :] = jnp.mean(kj, axis=0, keepdims=True)
            return carry
        lax.fori_loop(0, nb, mean_body, 0)

    NH = sa_sc.shape[0]
    heads = range(NH)

    def pair_lanes(h):
        return slice((h // 2) * LANES, (h // 2 + 1) * LANES)

    lane = _iota((BS, LANES), 1)
    first = lane < DHA
    q = q_ref[...].astype(F32) * (DHA ** -0.5 * LOG2E)
    qh = [jnp.where(first == (h % 2 == 0), q[:, pair_lanes(h)], 0.0).astype(BF16) for h in heads]

    ones = jnp.ones((2 * SUBLANES, GK), BF16)

    def values(g, h):
        return jnp.concatenate([vt_ref[g, h * DHA:(h + 1) * DHA, :], ones], axis=0)

    def bias_row(h, g, c):
        return bias_sc[h, pl.ds(GB * g + c, 1), :]

    def scores(g, h, dst):
        gk = jnp.minimum(g, ngroups - 1)
        rows = pl.ds(pl.multiple_of(gk * GK, GK), GK)
        s = _dot_nt(k_ref[rows, pair_lanes(h)], qh[h])
        mx = []
        for c in range(GB):
            sc = s[c * BS:(c + 1) * BS]
            dst[h, c] = sc
            mx.append(jnp.max(sc, axis=0, keepdims=True))
        return mx

    def weights(g, h, src, mx, state):
        m_old, acc = state
        bias = [bias_row(h, g, c) for c in range(GB)]
        mn = m_old
        for c in range(GB):
            mn = jnp.maximum(mn, mx[c] + bias[c])
        p = [jnp.exp2(src[h, c] - (mn - bias[c])).astype(BF16) for c in range(GB)]
        pv = _dot(values(jnp.minimum(g, ngroups - 1), h), jnp.concatenate(p, axis=0))
        return mn, acc * jnp.exp2(m_old - mn) + pv

    kmb = km_sc[...].astype(BF16)
    gate = [_dot_nt(kmb[:, pair_lanes(h)], qh[h]) for h in heads]
    own = pl.ds(pl.multiple_of(qi * BS, BS), BS)
    s_own = [_dot_nt(k_ref[own, pair_lanes(h)], qh[h]) for h in heads]
    mx_a = [scores(0, h, sa_sc) for h in heads]

    blk = _iota((nb, BS), 0)
    blkf = blk.astype(F32)
    for h in heads:
        g = jnp.where(blk < qi, gate[h], -jnp.inf)
        picked = jnp.zeros((nb, BS), jnp.bool_)
        for _ in range(min(MOBA_TOPK, nb - 1)):
            mx = jnp.max(g, axis=0, keepdims=True)
            idx = jnp.min(jnp.where(g == mx, blkf, float(nb)), axis=0, keepdims=True)
            pick = blkf == idx
            picked = jnp.logical_or(picked, pick)
            g = jnp.where(pick, -jnp.inf, g)
        bias_sc[h, 0:nb] = jnp.where(jnp.logical_and(picked, blk < qi), 0.0, NEG)
        bias_sc[h, nb:nb + SUBLANES] = jnp.full((SUBLANES, BS), NEG, F32)

    key_pos = _iota((BS, BS), 0)
    query_pos = _iota((BS, BS), 1)
    causal = key_pos <= query_pos
    own_cols = pl.ds(pl.multiple_of((qi % GB) * BS, BS), BS)
    state = []
    for h in heads:
        s = jnp.where(causal, s_own[h], NEG)
        mn = jnp.max(s, axis=0, keepdims=True)
        p = jnp.exp2(s - mn).astype(BF16)
        v_own = jnp.concatenate([vt_ref[qi // GB, h * DHA:(h + 1) * DHA, own_cols], ones[:, 0:BS]], axis=0)
        state.append((mn, _dot(v_own, p)))

    def past_step(t, carry):
        state, mx_a = carry
        g = 2 * t
        mx_b, mx_c, state = list(mx_a), list(mx_a), list(state)
        for h in heads:
            mx_b[h] = scores(g + 1, h, sb_sc)
            state[h] = weights(g, h, sa_sc, mx_a[h], state[h])
        for h in heads:
            mx_c[h] = scores(g + 2, h, sa_sc)
            state[h] = weights(g + 1, h, sb_sc, mx_b[h], state[h])
        return tuple(state), tuple(mx_c)

    full_steps = qi // (2 * GB)
    state, mx_a = lax.fori_loop(0, full_steps, past_step, (tuple(state), tuple(mx_a)))

    g_tail = 2 * full_steps
    left = qi - 2 * GB * full_steps

    def tail_one(carry):
        state, mx_a = carry
        return tuple(weights(g_tail, h, sa_sc, mx_a[h], state[h]) for h in heads)

    def tail_two(carry):
        state, mx_a = carry
        mx_b, state = list(mx_a), list(state)
        for h in heads:
            mx_b[h] = scores(g_tail + 1, h, sb_sc)
            state[h] = weights(g_tail, h, sa_sc, mx_a[h], state[h])
        return tuple(weights(g_tail + 1, h, sb_sc, mx_b[h], state[h]) for h in heads)

    def tail_any(carry):
        return lax.cond(left > GB, tail_two, tail_one, carry)

    state = lax.cond(left > 0, tail_any, lambda carry: carry[0], (state, mx_a))
    out_t = jnp.concatenate([state[h][1][0:DHA] / state[h][1][DHA:DHA + 1] for h in heads], axis=0)
    out_ref[...] = out_t.T.astype(BF16)


def _moba(pa, vt):
    bsz, s, _ = pa.shape
    BS = MOBA_BLOCK
    nh = MOBA_STEP_HEADS
    w = nh * MOBA_HEAD_DIM
    nstep = MOBA_HEADS // nh
    nb = s // BS
    gb = MOBA_GROUP
    assert nb % gb == 0 and gb <= SUBLANES and w % LANES == 0
    return pl.pallas_call(
        _moba_kernel,
        out_shape=jax.ShapeDtypeStruct((bsz, s, nstep * w), BF16),
        grid=(bsz, nstep, nb),
        in_specs=[pl.BlockSpec((None, BS, w), lambda b, p, i: (b, i, p)),
                  pl.BlockSpec((None, s, w), lambda b, p, i: (b, 0, nstep + p)),
                  pl.BlockSpec((None, nb // gb, w, gb * BS), lambda b, p, i: (b, 0, p, 0))],
        out_specs=pl.BlockSpec((None, BS, w), lambda b, p, i: (b, i, p)),
        scratch_shapes=[pltpu.VMEM((nb, w), F32), pltpu.VMEM((nh, nb + SUBLANES, BS), F32),
                        pltpu.VMEM((nh, gb, BS, BS), F32), pltpu.VMEM((nh, gb, BS, BS), F32)],
        compiler_params=pltpu.CompilerParams(dimension_semantics=("arbitrary",) * 3,
                                             vmem_limit_bytes=VMEM_LIMIT),
        name="moba",
    )(pa, pa, vt)


def _outproj_kernel(hm_ref, ha_ref, x_ref, gate_ref, shift_ref, scale_ref, g_ref, wo_ref, wr_ref, br_ref,
                    x1_ref, hn_ref, ri_ref, cnt_ref):
    dm = hm_ref.shape[1]
    mix = _dot(hm_ref[...], wo_ref[0:dm, :]) + _dot(ha_ref[...], wo_ref[dm:, :])
    x1 = x_ref[...] + gate_ref[...] * mix
    x1_ref[...] = x1
    ms = jnp.mean(x1 * x1, axis=-1, keepdims=True)
    hn = x1 * lax.rsqrt(ms + EPS) * g_ref[...] * (1.0 + scale_ref[...]) + shift_ref[...]
    _store_rows(hn_ref, hn)
    logits = _dot(hn.astype(BF16), wr_ref[...]) + br_ref[...]

    tm = logits.shape[0]
    lane = _iota((tm, LANES), 1)
    lanef = lane.astype(F32)
    big = float(LANES)
    in_groups = lane < N_GROUPS
    gl = jnp.where(in_groups, logits, -jnp.inf)
    gmax = jnp.max(gl, axis=-1, keepdims=True)
    gidx = jnp.min(jnp.where(gl == gmax, lanef, big), axis=-1, keepdims=True)
    g_w = 1.0 / jnp.sum(jnp.where(in_groups, jnp.exp(gl - gmax), 0.0), axis=-1, keepdims=True)
    lo = N_GROUPS + EXPERTS_PER_GROUP * gidx
    el = jnp.where(jnp.logical_and(lanef >= lo, lanef < lo + EXPERTS_PER_GROUP), logits, -jnp.inf)
    e1 = jnp.max(el, axis=-1, keepdims=True)
    i1 = jnp.min(jnp.where(el == e1, lanef, big), axis=-1, keepdims=True)
    el = jnp.where(lanef == i1, -jnp.inf, el)
    e2 = jnp.max(el, axis=-1, keepdims=True)
    i2 = jnp.min(jnp.where(el == e2, lanef, big), axis=-1, keepdims=True)
    t = jnp.exp(e2 - e1)
    w1 = g_w / (1.0 + t)
    w2 = g_w * t / (1.0 + t)
    ri = jnp.where(lane == 0, i1 - N_GROUPS,
                   jnp.where(lane == 1, i2 - N_GROUPS,
                             jnp.where(lane == 2, w1, jnp.where(lane == 3, w2, 0.0))))
    ri_ref[...] = ri

    @pl.when(jnp.logical_and(pl.program_id(0) == 0, pl.program_id(1) == 0))
    def _():
        cnt_ref[...] = jnp.zeros_like(cnt_ref)
    picked = jnp.logical_or(lanef == i1 - N_GROUPS, lanef == i2 - N_GROUPS)
    cnt_ref[...] = cnt_ref[...] + jnp.sum(jnp.where(picked, 1.0, 0.0), axis=0, keepdims=True)


def _outproj(hm, ha, x, mod, g2, wo, wr, br):
    bsz, s, d = x.shape
    tm = min(TOKEN_TILE, s)
    dh = hm.shape[2]
    const = lambda b, i: (0, 0)
    tile = lambda w: pl.BlockSpec((None, tm, w), lambda b, i: (b, i, 0))
    modrow = lambda k: pl.BlockSpec((None, None, 1, d), lambda b, i: (b, k, 0, 0))
    return pl.pallas_call(
        _outproj_kernel,
        out_shape=(jax.ShapeDtypeStruct((bsz, s, d), F32),
                   jax.ShapeDtypeStruct((bsz, s * SUBLANES, LANES), F32),
                   jax.ShapeDtypeStruct((bsz, s, LANES), F32),
                   jax.ShapeDtypeStruct((SUBLANES, LANES), F32)),
        grid=(bsz, s // tm),
        in_specs=[tile(dh), tile(dh), tile(d), modrow(2), modrow(3), modrow(4),
                  pl.BlockSpec((1, d), const),
                  pl.BlockSpec((2 * dh, d), const),
                  pl.BlockSpec((d, LANES), const),
                  pl.BlockSpec((1, LANES), const)],
        out_specs=(tile(d), pl.BlockSpec((None, tm * SUBLANES, LANES), lambda b, i: (b, i, 0)), tile(LANES),
                   pl.BlockSpec((SUBLANES, LANES), const)),
        compiler_params=pltpu.CompilerParams(dimension_semantics=("arbitrary", "arbitrary"),
                                             vmem_limit_bytes=VMEM_LIMIT),
        name="outproj",
    )(hm, ha, x, mod, mod, mod, g2, wo, wr, br)


def _rank_kernel(ri_ref, cnt_ref, dest_ref, run_sc):
    tm = ri_ref.shape[0]
    lane = _iota((tm, LANES), 1).astype(F32)
    ri = ri_ref[...]
    oh0 = lane == ri[:, 0:1]
    oh1 = lane == ri[:, 1:2]
    oh = jnp.where(jnp.logical_or(oh0, oh1), 1.0, 0.0)

    @pl.when(pl.program_id(0) == 0)
    def _():
        padded = jnp.ceil(cnt_ref[...] / ROW_BLOCK) * ROW_BLOCK
        l8 = _iota((SUBLANES, LANES), 1)
        acc = padded
        k = 1
        while k < LANES:
            acc = acc + jnp.where(l8 >= k, pltpu.roll(acc, k, 1), 0.0)
            k *= 2
        run_sc[...] = acc - padded

    row = _iota((tm, tm), 0)
    col = _iota((tm, tm), 1)
    before = _dot((col < row).astype(BF16), oh.astype(BF16))
    tot = before + run_sc[0:1, :]
    d0 = jnp.sum(jnp.where(oh0, tot, 0.0), axis=-1, keepdims=True)
    d1 = jnp.sum(jnp.where(oh1, tot, 0.0), axis=-1, keepdims=True)
    lane_i = _iota((tm, LANES), 1)
    dest_ref[...] = jnp.where(lane_i == 0, d0, jnp.where(lane_i == 1, d1, 0.0))
    run_sc[...] = run_sc[...] + jnp.sum(oh, axis=0, keepdims=True)


def _rank(ri, cnt):
    t = ri.shape[0]
    tm = min(TOKEN_TILE, t)
    return pl.pallas_call(
        _rank_kernel,
        out_shape=jax.ShapeDtypeStruct((t, LANES), F32),
        grid=(t // tm,),
        in_specs=[pl.BlockSpec((tm, LANES), lambda i: (i, 0)),
                  pl.BlockSpec((SUBLANES, LANES), lambda i: (0, 0))],
        out_specs=pl.BlockSpec((tm, LANES), lambda i: (i, 0)),
        scratch_shapes=[pltpu.VMEM((SUBLANES, LANES), F32)],
        compiler_params=pltpu.CompilerParams(dimension_semantics=("arbitrary",)),
        name="rank",
    )(ri, cnt)


def _dispatch_kernel(pe_ref, nu_ref, dest_ref, hn_ref, xpad_ref, zbuf, sem, zsem):
    tm = hn_ref.shape[0] // SUBLANES
    block_rows = ROW_BLOCK * SUBLANES
    nblk = xpad_ref.shape[0] // block_rows

    def zero_block(j):
        rows = pl.ds(pl.multiple_of(j * block_rows, block_rows), block_rows)
        return pltpu.make_async_copy(zbuf, xpad_ref.at[rows, :], zsem)

    @pl.when(pl.program_id(0) == 0)
    def _():
        zbuf[...] = jnp.zeros_like(zbuf)

        def tail(e, n):
            first = jnp.where(e == 0, 0, pe_ref[jnp.maximum(e - 1, 0)])
            nonempty = pe_ref[e] > first

            @pl.when(nonempty)
            def _():
                zero_block(pe_ref[e] - 1).start()
            return n + nonempty.astype(jnp.int32)
        started = lax.fori_loop(0, N_EXPERTS, tail, 0)

        def unused(j, c):
            zero_block(j).start()
            return c
        lax.fori_loop(nu_ref[0], nblk, unused, 0)

        def drain(j, c):
            zero_block(0).wait()
            return c
        lax.fori_loop(0, started + nblk - nu_ref[0], drain, 0)

    def row_copy(t, k):
        d = dest_ref[0, 2 * t + k]
        return pltpu.make_async_copy(_token_tile(hn_ref, t), _token_tile(xpad_ref, d), sem)

    def issue(t, carry):
        row_copy(t, 0).start(priority=0)
        row_copy(t, 1).start(priority=1)
        return carry
    lax.fori_loop(0, tm, issue, 0, unroll=ROW_DMA_UNROLL)

    for _ in range(2):
        pltpu.make_async_copy(hn_ref, xpad_ref.at[pl.ds(0, tm * SUBLANES), :], sem).wait()


def _dispatch(pad_ends, n_used, dest, hn, rows):
    t = hn.shape[0] // SUBLANES
    tm = min(TOKEN_TILE, t)
    return pl.pallas_call(
        _dispatch_kernel,
        out_shape=jax.ShapeDtypeStruct((rows * SUBLANES, LANES), hn.dtype),
        grid_spec=pltpu.PrefetchScalarGridSpec(
            num_scalar_prefetch=2,
            grid=(t // tm,),
            in_specs=[pl.BlockSpec((None, 1, 2 * tm), lambda i, pe, nu: (i, 0, 0), memory_space=pltpu.SMEM),
                      pl.BlockSpec((tm * SUBLANES, LANES), lambda i, pe, nu: (i, 0))],
            out_specs=pl.BlockSpec(memory_space=pl.ANY),
            scratch_shapes=[pltpu.VMEM((ROW_BLOCK * SUBLANES, LANES), hn.dtype),
                            pltpu.SemaphoreType.DMA, pltpu.SemaphoreType.DMA]),
        compiler_params=pltpu.CompilerParams(dimension_semantics=("arbitrary",)),
        name="dispatch",
    )(pad_ends, n_used, dest.reshape(t // tm, 1, 2 * tm), hn)


def _expert_kernel(be_ref, nu_ref, x_ref, wg_ref, wu_ref, wd_ref, y_ref, wgb, wub, wdb):
    i = pl.program_id(0)
    live = i < nu_ref[0]
    new_expert = jnp.logical_or(i == 0, be_ref[i] != be_ref[jnp.maximum(i - 1, 0)])

    @pl.when(jnp.logical_and(live, new_expert))
    def _():
        wgb[...] = wg_ref[...].astype(BF16)
        wub[...] = wu_ref[...].astype(BF16)
        wdb[...] = wd_ref[...].astype(BF16)

    @pl.when(live)
    def _():
        xb = _load_rows(x_ref).astype(BF16)
        g = _dot(xb, wgb[...])
        u = _dot(xb, wub[...])
        _store_rows(y_ref, _dot((g * _sigmoid(g) * u).astype(BF16), wdb[...]))

    @pl.when(jnp.logical_not(live))
    def _():
        y_ref[...] = jnp.zeros_like(y_ref)


def _experts(blk_expert, n_used, xpad, wg, wu, wd):
    rows = xpad.shape[0] // SUBLANES
    d, de = wg.shape[1], wg.shape[2]
    assert d == SUBLANES * LANES
    row_block = (ROW_BLOCK * SUBLANES, LANES)
    nblk = rows // ROW_BLOCK
    live = lambda i, be, nu: jnp.minimum(i, nu[0] - 1)
    return pl.pallas_call(
        _expert_kernel,
        out_shape=jax.ShapeDtypeStruct(xpad.shape, F32),
        grid_spec=pltpu.PrefetchScalarGridSpec(
            num_scalar_prefetch=2,
            grid=(nblk,),
            in_specs=[pl.BlockSpec(row_block, lambda i, be, nu: (live(i, be, nu), 0)),
                      pl.BlockSpec((None, d, de), lambda i, be, nu: (be[live(i, be, nu)], 0, 0)),
                      pl.BlockSpec((None, d, de), lambda i, be, nu: (be[live(i, be, nu)], 0, 0)),
                      pl.BlockSpec((None, de, d), lambda i, be, nu: (be[live(i, be, nu)], 0, 0))],
            out_specs=pl.BlockSpec(row_block, lambda i, be, nu: (i, 0)),
            scratch_shapes=[pltpu.VMEM((d, de), BF16), pltpu.VMEM((d, de), BF16), pltpu.VMEM((de, d), BF16)]),
        compiler_params=pltpu.CompilerParams(dimension_semantics=("arbitrary",),
                                             vmem_limit_bytes=VMEM_LIMIT),
        name="experts",
    )(blk_expert, n_used, xpad, wg, wu, wd)


def _combine_kernel(final_norm, dest_ref, dnext_ref, ypad_ref, x1_ref, ri_ref, gate_ref, gf_ref, out_ref, ybuf, sem):
    tm = x1_ref.shape[0]
    step = pl.program_id(0) * pl.num_programs(1) + pl.program_id(1)
    nsteps = pl.num_programs(0) * pl.num_programs(1)
    slot = step % 2

    def gather(idx_ref, buf):
        def row_copy(t, k):
            d = idx_ref[0, 2 * t + k]
            return pltpu.make_async_copy(_token_tile(ypad_ref, d), _token_tile(ybuf.at[buf, k], t), sem.at[buf])

        def issue(t, carry):
            row_copy(t, 0).start(priority=0)
            row_copy(t, 1).start(priority=1)
            return carry
        lax.fori_loop(0, tm, issue, 0, unroll=ROW_DMA_UNROLL)

    @pl.when(step == 0)
    def _():
        gather(dest_ref, 0)

    @pl.when(step + 1 < nsteps)
    def _():
        gather(dnext_ref, 1 - slot)

    for k in range(2):
        pltpu.make_async_copy(ypad_ref.at[pl.ds(0, tm * SUBLANES), :], ybuf.at[slot, k], sem.at[slot]).wait()

    ri = ri_ref[...]
    ffn = ri[:, 2:3] * _load_rows(ybuf.at[slot, 0]) + ri[:, 3:4] * _load_rows(ybuf.at[slot, 1])
    x2 = x1_ref[...] + gate_ref[...] * ffn
    if final_norm:
        ms = jnp.mean(x2 * x2, axis=-1, keepdims=True)
        x2 = x2 * lax.rsqrt(ms + EPS) * gf_ref[...]
    out_ref[...] = x2


def _combine(dest, ypad, x1, ri, mod, gf, final_norm):
    bsz, s, d = x1.shape
    tm = min(COMBINE_TILE, s)
    nt = s // tm
    dest3 = dest.reshape(bsz * nt, 1, 2 * tm)
    return pl.pallas_call(
        functools.partial(_combine_kernel, final_norm),
        out_shape=jax.ShapeDtypeStruct((bsz, s, d), F32),
        grid=(bsz, nt),
        in_specs=[pl.BlockSpec((None, 1, 2 * tm), lambda b, i: (b * nt + i, 0, 0), memory_space=pltpu.SMEM),
                  pl.BlockSpec((None, 1, 2 * tm), lambda b, i: (jnp.minimum(b * nt + i + 1, bsz * nt - 1), 0, 0),
                               memory_space=pltpu.SMEM),
                  pl.BlockSpec(memory_space=pl.ANY),
                  pl.BlockSpec((None, tm, d), lambda b, i: (b, i, 0)),
                  pl.BlockSpec((None, tm, LANES), lambda b, i: (b, i, 0)),
                  pl.BlockSpec((None, None, 1, d), lambda b, i: (b, 5, 0, 0)),
                  pl.BlockSpec((1, d), lambda b, i: (0, 0))],
        out_specs=pl.BlockSpec((None, tm, d), lambda b, i: (b, i, 0)),
        scratch_shapes=[pltpu.VMEM((2, 2, tm * SUBLANES, LANES), F32), pltpu.SemaphoreType.DMA((2,))],
        compiler_params=pltpu.CompilerParams(dimension_semantics=("arbitrary", "arbitrary"),
                                             vmem_limit_bytes=VMEM_LIMIT),
        name="combine",
    )(dest3, dest3, ypad, x1, ri.reshape(bsz, s, LANES), mod, gf)


def _pad_lanes(a):
    return jnp.pad(a, ((0, 0), (0, LANES - a.shape[1])))


def kernel(x, c, w_ada, b_ada, g_norm1, w_in, w_conv, b_conv, b_gates, g_mlstm_head, w_out, g_norm2,
           w_router_group, b_router_group, w_router_expert, b_router_expert, w_expert_gate, w_expert_up,
           w_expert_down, g_final):
    bsz, s, d = x.shape
    t = bsz * s
    dm = MLSTM_HEADS * MLSTM_HEAD_DIM
    da = MOBA_HEADS * MOBA_HEAD_DIM
    ng = 2 * MLSTM_HEADS
    n_assign = 2 * t
    n_blocks = -(-n_assign // ROW_BLOCK) + N_EXPERTS
    rows = n_blocks * ROW_BLOCK

    for l in range(w_ada.shape[0]):
        mod = _adaln(c, w_ada[l], b_ada[l]).reshape(bsz, 6, 1, d)

        w = w_in[l]
        wm = w[:, :4 * dm].astype(BF16)
        wgates = w[:, 4 * dm:4 * dm + ng]
        wa = w[:, 4 * dm + ng:4 * dm + ng + 2 * da].astype(BF16)
        wvt = w[:, 4 * dm + ng + 2 * da:].T.astype(BF16)
        pm, pa, vt, gc, gt = _inproj(x, mod, g_norm1[l].reshape(1, d), wm, wa, wvt,
                                     _pad_lanes(wgates).astype(BF16), wgates.T.astype(BF16),
                                     _pad_lanes(b_gates[l].reshape(1, ng)), b_gates[l].reshape(ng, 1))
        hm = _mlstm(pm, gc, gt, w_conv[l], b_conv[l], g_mlstm_head[l])
        ha = _moba(pa, vt)

        wr = _pad_lanes(jnp.concatenate([w_router_group[l], w_router_expert[l]], axis=1)).astype(BF16)
        br = _pad_lanes(jnp.concatenate([b_router_group[l], b_router_expert[l]]).reshape(1, -1))
        x1, hn2, ri, cnt = _outproj(hm, ha, x, mod, g_norm2[l].reshape(1, d), w_out[l].astype(BF16), wr, br)
        ri = ri.reshape(t, LANES)
        destf = _rank(ri, cnt)
        dest = destf[:, :2].astype(jnp.int32).reshape(n_assign)
        counts = cnt[0, :N_EXPERTS].astype(jnp.int32)
        pad_ends = jnp.cumsum((counts + ROW_BLOCK - 1) // ROW_BLOCK)
        blk_ids = jnp.arange(n_blocks, dtype=jnp.int32)
        blk_expert = jnp.minimum(jnp.sum((pad_ends[None, :] <= blk_ids[:, None]).astype(jnp.int32), axis=1),
                                 N_EXPERTS - 1)
        n_used = pad_ends[-1:].astype(jnp.int32)
        xpad = _dispatch(pad_ends.astype(jnp.int32), n_used, dest, hn2.reshape(t * SUBLANES, LANES), rows)
        ypad = _experts(blk_expert, n_used, xpad, w_expert_gate[l], w_expert_up[l], w_expert_down[l])
        x = _combine(dest, ypad, x1, ri, mod, g_final.reshape(1, d), l == w_ada.shape[0] - 1)
    return x
```

```python
import functools

import jax
import jax.numpy as jnp
from jax import lax
from jax.experimental import pallas as pl
from jax.experimental.pallas import tpu as pltpu

F32 = jnp.float32
BF16 = jnp.bfloat16

MLSTM_HEADS = 4
MLSTM_HEAD_DIM = 128
CONV_WIDTH = 4
MOBA_HEADS = 8
MOBA_HEAD_DIM = 64
MOBA_BLOCK = 256
MOBA_TOPK = 3
N_GROUPS = 4
EXPERTS_PER_GROUP = 8
N_EXPERTS = N_GROUPS * EXPERTS_PER_GROUP
EPS = 1e-6

LANES = 128
SUBLANES = 8
ROW_BLOCK = 512
MLSTM_CHUNK = 256
TOKEN_TILE = 512
DISPATCH_TILE = 1024
COMBINE_TILE = 256
ROW_DMA_UNROLL = 8
NEG = -1e30
LOG2E = 1.4426950408889634
MOBA_GROUP = 2
MOBA_GROUP_KEYS = MOBA_GROUP * MOBA_BLOCK
MOBA_STEP_HEADS = 8
VMEM_LIMIT = 48 * 1024 * 1024


def _dot(a, b):
    return jnp.dot(a, b, preferred_element_type=F32)


def _dot_nt(a, b):
    return lax.dot_general(a, b, (((1,), (1,)), ((), ())), preferred_element_type=F32)


def _dot_tn(a, b):
    return lax.dot_general(a, b, (((0,), (0,)), ((), ())), preferred_element_type=F32)


def _sigmoid(x):
    return 1.0 / (1.0 + jnp.exp(-x))


def _log_sigmoid(x):
    return jnp.minimum(x, 0.0) - jnp.log1p(jnp.exp(-jnp.abs(x)))


def _split3(x):
    hi = x.astype(BF16)
    r1 = x - hi.astype(F32)
    mid = r1.astype(BF16)
    lo = (r1 - mid.astype(F32)).astype(BF16)
    return hi, mid, lo


def _iota(shape, dim):
    return lax.broadcasted_iota(jnp.int32, shape, dim)


def _load_rows(ref):
    n = ref.shape[0] // SUBLANES
    return jnp.concatenate([ref[pl.ds(j, n, stride=SUBLANES), :] for j in range(SUBLANES)], axis=1)


def _store_rows(ref, val):
    n = ref.shape[0] // SUBLANES
    for j in range(SUBLANES):
        ref[pl.ds(j, n, stride=SUBLANES), :] = val[:, j * LANES:(j + 1) * LANES]


def _token_tile(ref, t):
    return ref.at[pl.ds(pl.multiple_of(t * SUBLANES, SUBLANES), SUBLANES), :]


def _adaln_kernel(c_ref, w_ref, b_ref, o_ref):
    c = c_ref[...]
    o_ref[...] = _dot(c * _sigmoid(c), w_ref[...]) + b_ref[...]


def _adaln(c, w, b):
    bsz, d = c.shape
    n = w.shape[1]
    tn = 1024
    return pl.pallas_call(
        _adaln_kernel,
        out_shape=jax.ShapeDtypeStruct((bsz, n), F32),
        grid=(n // tn,),
        in_specs=[pl.BlockSpec((bsz, d), lambda j: (0, 0)),
                  pl.BlockSpec((d, tn), lambda j: (0, j)),
                  pl.BlockSpec((1, tn), lambda j: (0, j))],
        out_specs=pl.BlockSpec((bsz, tn), lambda j: (0, j)),
        name="adaln",
    )(c, w, b.reshape(1, n))


def _inproj_kernel(x_ref, shift_ref, scale_ref, g_ref, wm_ref, wa_ref, wvt_ref, wg_ref, wgt_ref, bg_ref, bgt_ref,
                   pm_ref, pa_ref, vt_ref, gc_ref, gt_ref):
    x = x_ref[...]
    ms = jnp.mean(x * x, axis=-1, keepdims=True)
    y = x * lax.rsqrt(ms + EPS) * g_ref[...]
    hb = (y * (1.0 + scale_ref[...]) + shift_ref[...]).astype(BF16)
    cw = 512
    for j in range(wm_ref.shape[1] // cw):
        pm_ref[:, j * cw:(j + 1) * cw] = _dot(hb, wm_ref[:, j * cw:(j + 1) * cw])
    for j in range(wa_ref.shape[1] // cw):
        pa_ref[:, j * cw:(j + 1) * cw] = _dot(hb, wa_ref[:, j * cw:(j + 1) * cw]).astype(BF16)
    vt = _dot_nt(wvt_ref[...], hb).astype(BF16)
    for j in range(vt_ref.shape[0]):
        vt_ref[j] = vt[:, j * MOBA_GROUP_KEYS:(j + 1) * MOBA_GROUP_KEYS]
    gc_ref[...] = _dot(hb, wg_ref[...]) + bg_ref[...]
    gt_ref[...] = _dot_nt(wgt_ref[...], hb) + bgt_ref[...]


def _inproj(x, mod, g1, wm, wa, wvt, wg, wgt, bg, bgt):
    bsz, s, d = x.shape
    tm = min(TOKEN_TILE, s)
    nm, na, nv = wm.shape[1], wa.shape[1], wvt.shape[0]
    vb = tm // MOBA_GROUP_KEYS
    ng = wgt.shape[0]
    const = lambda b, i: (0, 0)
    return pl.pallas_call(
        _inproj_kernel,
        out_shape=(jax.ShapeDtypeStruct((bsz, s, nm), F32),
                   jax.ShapeDtypeStruct((bsz, s, na), BF16),
                   jax.ShapeDtypeStruct((bsz, s // MOBA_GROUP_KEYS, nv, MOBA_GROUP_KEYS), BF16),
                   jax.ShapeDtypeStruct((bsz, s, LANES), F32),
                   jax.ShapeDtypeStruct((bsz, ng, s), F32)),
        grid=(bsz, s // tm),
        in_specs=[pl.BlockSpec((None, tm, d), lambda b, i: (b, i, 0)),
                  pl.BlockSpec((None, None, 1, d), lambda b, i: (b, 0, 0, 0)),
                  pl.BlockSpec((None, None, 1, d), lambda b, i: (b, 1, 0, 0)),
                  pl.BlockSpec((1, d), const),
                  pl.BlockSpec((d, nm), const),
                  pl.BlockSpec((d, na), const),
                  pl.BlockSpec((nv, d), const),
                  pl.BlockSpec((d, LANES), const),
                  pl.BlockSpec((ng, d), const),
                  pl.BlockSpec((1, LANES), const),
                  pl.BlockSpec((ng, 1), const)],
        out_specs=(pl.BlockSpec((None, tm, nm), lambda b, i: (b, i, 0)),
                   pl.BlockSpec((None, tm, na), lambda b, i: (b, i, 0)),
                   pl.BlockSpec((None, vb, nv, MOBA_GROUP_KEYS), lambda b, i: (b, i, 0, 0)),
                   pl.BlockSpec((None, tm, LANES), lambda b, i: (b, i, 0)),
                   pl.BlockSpec((None, ng, tm), lambda b, i: (b, 0, i))),
        compiler_params=pltpu.CompilerParams(dimension_semantics=("arbitrary", "arbitrary"),
                                             vmem_limit_bytes=VMEM_LIMIT),
        name="inproj",
    )(x, mod, mod, g1, wm, wa, wvt, wg, wgt, bg, bgt)


def _mlstm_kernel(q_ref, k_ref, v_ref, o_ref, gc_ref, gt_ref, wc_ref, bc_ref, gh_ref, out_ref,
                  qbuf, kbuf, c_sc, n_sc, m_sc):
    L = q_ref.shape[0]
    H, DH = MLSTM_HEADS, MLSTM_HEAD_DIM
    HD = H * DH
    ci = pl.program_id(1)

    @pl.when(ci == 0)
    def _():
        qbuf[0:SUBLANES, :] = jnp.zeros((SUBLANES, HD), F32)
        kbuf[0:SUBLANES, :] = jnp.zeros((SUBLANES, HD), F32)
        c_sc[...] = jnp.zeros_like(c_sc)
        n_sc[...] = jnp.zeros_like(n_sc)
        m_sc[...] = jnp.zeros_like(m_sc)

    qbuf[SUBLANES:SUBLANES + L, :] = q_ref[...]
    kbuf[SUBLANES:SUBLANES + L, :] = k_ref[...]
    qc = jnp.broadcast_to(bc_ref[:, 0:HD], (L, HD))
    kc = jnp.broadcast_to(bc_ref[:, HD:2 * HD], (L, HD))
    for j in range(CONV_WIDTH):
        off = SUBLANES - (CONV_WIDTH - 1) + j
        qc = qc + wc_ref[j:j + 1, 0:HD] * qbuf[off:off + L, :]
        kc = kc + wc_ref[j:j + 1, HD:2 * HD] * kbuf[off:off + L, :]
    qbuf[0:SUBLANES, :] = qbuf[L:L + SUBLANES, :]
    kbuf[0:SUBLANES, :] = kbuf[L:L + SUBLANES, :]
    qa = qc * _sigmoid(qc) * (DH ** -0.5)
    ka = kc * _sigmoid(kc)

    gc = gc_ref[...]
    gt = gt_ref[...]
    row = _iota((L, L), 0)
    col = _iota((L, L), 1)
    causal = col <= row
    tril = causal.astype(BF16)
    triu = (row <= col).astype(BF16)
    h3 = _split3(_log_sigmoid(gc))
    cum_c = _dot(tril, h3[0]) + _dot(tril, h3[1]) + _dot(tril, h3[2])
    r3 = _split3(_log_sigmoid(gt))
    cum_r = _dot(r3[0], triu) + _dot(r3[1], triu) + _dot(r3[2], triu)

    for h in range(H):
        sl = slice(h * DH, (h + 1) * DH)
        qf = qa[:, sl]
        qb = qf.astype(BF16)
        kf = ka[:, sl]
        kb = kf.astype(BF16)
        vb = v_ref[:, sl].astype(BF16)
        a_c = cum_c[:, H + h:H + h + 1]
        li_c = gc[:, h:h + 1]
        b_r = gt[h:h + 1, :] - cum_r[H + h:H + h + 1, :]
        cum_last = cum_r[H + h:H + h + 1, L - 1:L]
        m_prev = m_sc[h:h + 1, 0:1]
        c_prev = c_sc[h]
        n_prev = n_sc[h:h + 1, :]

        dm = jnp.where(causal, a_c + b_r, -jnp.inf)
        inter = a_c + m_prev
        m_t = jnp.maximum(inter, jnp.max(dm, axis=-1, keepdims=True))
        w_inter = jnp.exp(inter - m_t)
        sc = _dot_nt(qb, kb) * jnp.exp(dm - m_t)
        num = w_inter * _dot(qb, c_prev.astype(BF16)) + _dot(sc.astype(BF16), vb)
        den = w_inter * jnp.sum(qf * n_prev, axis=-1, keepdims=True) + jnp.sum(sc, axis=-1, keepdims=True)
        ht = num / jnp.maximum(jnp.abs(den), jnp.exp(-m_t))

        g_end_r = cum_last + b_r
        m_new = jnp.maximum(cum_last + m_prev, jnp.max(g_end_r, axis=-1, keepdims=True))
        w_old = jnp.exp(cum_last + m_prev - m_new)
        w_new_c = jnp.exp(cum_last + li_c - a_c - m_new)
        kw = kf * w_new_c
        c_sc[h] = w_old * c_prev + _dot_tn(kw.astype(BF16), vb)
        n_sc[h:h + 1, :] = w_old * n_prev + jnp.sum(kw, axis=0, keepdims=True)
        m_sc[h:h + 1, :] = jnp.broadcast_to(m_new, (1, LANES))

        hn = ht * lax.rsqrt(jnp.mean(ht * ht, axis=-1, keepdims=True) + EPS) * gh_ref[:, sl]
        out_ref[:, sl] = (hn * _sigmoid(o_ref[:, sl])).astype(BF16)


def _mlstm(pm, gc, gt, w_conv, b_conv, g_head):
    bsz, s, _ = pm.shape
    L = min(MLSTM_CHUNK, s)
    H, DH = MLSTM_HEADS, MLSTM_HEAD_DIM
    HD = H * DH
    const = lambda b, i: (0, 0)
    col_block = lambda j: pl.BlockSpec((None, L, HD), lambda b, i: (b, i, j))
    return pl.pallas_call(
        _mlstm_kernel,
        out_shape=jax.ShapeDtypeStruct((bsz, s, HD), BF16),
        grid=(bsz, s // L),
        in_specs=[col_block(0), col_block(1), col_block(2), col_block(3),
                  pl.BlockSpec((None, L, LANES), lambda b, i: (b, i, 0)),
                  pl.BlockSpec((None, 2 * H, L), lambda b, i: (b, 0, i)),
                  pl.BlockSpec((CONV_WIDTH, 2 * HD), const),
                  pl.BlockSpec((1, 2 * HD), const),
                  pl.BlockSpec((1, HD), const)],
        out_specs=pl.BlockSpec((None, L, HD), lambda b, i: (b, i, 0)),
        scratch_shapes=[pltpu.VMEM((L + SUBLANES, HD), F32),
                        pltpu.VMEM((L + SUBLANES, HD), F32),
                        pltpu.VMEM((H, DH, DH), F32),
                        pltpu.VMEM((SUBLANES, DH), F32),
                        pltpu.VMEM((SUBLANES, LANES), F32)],
        compiler_params=pltpu.CompilerParams(dimension_semantics=("arbitrary", "arbitrary"),
                                             vmem_limit_bytes=VMEM_LIMIT),
        name="mlstm",
    )(pm, pm, pm, pm, gc, gt, w_conv, b_conv.reshape(1, 2 * HD), g_head.reshape(1, HD))


def _moba_kernel(q_ref, k_ref, vt_ref, out_ref, km_sc, bias_sc, sa_sc, sb_sc):
    BS = MOBA_BLOCK
    DHA = MOBA_HEAD_DIM
    GB = MOBA_GROUP
    GK = GB * BS
    nb = k_ref.shape[0] // BS
    ngroups = nb // GB
    qi = pl.program_id(2)

    @pl.when(qi == 0)
    def _():
        def mean_body(j, carry):
            kj = k_ref[pl.ds(pl.multiple_of(j * BS, BS), BS), :].astype(F32)
            km_sc[pl.ds(j, 1), :] = jnp.mean(kj, axis=0, keepdims=True)
            return carry
        lax.fori_loop(0, nb, mean_body, 0)

    NH = sa_sc.shape[0]
    heads = range(NH)

    def pair_lanes(h):
        return slice((h // 2) * LANES, (h // 2 + 1) * LANES)

    lane = _iota((BS, LANES), 1)
    first = lane < DHA
    q = q_ref[...].astype(F32) * (DHA ** -0.5 * LOG2E)
    qh = [jnp.where(first == (h % 2 == 0), q[:, pair_lanes(h)], 0.0).astype(BF16) for h in heads]

    ones = jnp.ones((2 * SUBLANES, GK), BF16)

    def values(g, h):
        return jnp.concatenate([vt_ref[g, h * DHA:(h + 1) * DHA, :], ones], axis=0)

    def bias_row(h, g, c):
        return bias_sc[h, pl.ds(GB * g + c, 1), :]

    def scores(g, h, dst):
        gk = jnp.minimum(g, ngroups - 1)
        rows = pl.ds(pl.multiple_of(gk * GK, GK), GK)
        s = _dot_nt(k_ref[rows, pair_lanes(h)], qh[h])
        mx = []
        for c in range(GB):
            sc = s[c * BS:(c + 1) * BS]
            dst[h, c] = sc
            mx.append(jnp.max(sc, axis=0, keepdims=True))
        return mx

    def weights(g, h, src, mx, state):
        m_old, acc = state
        bias = [bias_row(h, g, c) for c in range(GB)]
        mn = m_old
        for c in range(GB):
            mn = jnp.maximum(mn, mx[c] + bias[c])
        p = [jnp.exp2(src[h, c] - (mn - bias[c])).astype(BF16) for c in range(GB)]
        pv = _dot(values(jnp.minimum(g, ngroups - 1), h), jnp.concatenate(p, axis=0))
        return mn, acc * jnp.exp2(m_old - mn) + pv

    kmb = km_sc[...].astype(BF16)
    gate = [_dot_nt(kmb[:, pair_lanes(h)], qh[h]) for h in heads]
    own = pl.ds(pl.multiple_of(qi * BS, BS), BS)
    s_own = [_dot_nt(k_ref[own, pair_lanes(h)], qh[h]) for h in heads]
    mx_a = [scores(0, h, sa_sc) for h in heads]

    blk = _iota((nb, BS), 0)
    blkf = blk.astype(F32)
    for h in heads:
        g = jnp.where(blk < qi, gate[h], -jnp.inf)
        picked = jnp.zeros((nb, BS), jnp.bool_)
        for _ in range(min(MOBA_TOPK, nb - 1)):
            mx = jnp.max(g, axis=0, keepdims=True)
            idx = jnp.min(jnp.where(g == mx, blkf, float(nb)), axis=0, keepdims=True)
            pick = blkf == idx
            picked = jnp.logical_or(picked, pick)
            g = jnp.where(pick, -jnp.inf, g)
        bias_sc[h, 0:nb] = jnp.where(jnp.logical_and(picked, blk < qi), 0.0, NEG)
        bias_sc[h, nb:nb + SUBLANES] = jnp.full((SUBLANES, BS), NEG, F32)

    key_pos = _iota((BS, BS), 0)
    query_pos = _iota((BS, BS), 1)
    causal = key_pos <= query_pos
    own_cols = pl.ds(pl.multiple_of((qi % GB) * BS, BS), BS)
    state = []
    for h in heads:
        s = jnp.where(causal, s_own[h], NEG)
        mn = jnp.max(s, axis=0, keepdims=True)
        p = jnp.exp2(s - mn).astype(BF16)
        v_own = jnp.concatenate([vt_ref[qi // GB, h * DHA:(h + 1) * DHA, own_cols], ones[:, 0:BS]], axis=0)
        state.append((mn, _dot(v_own, p)))

    def past_step(t, carry):
        state, mx_a = carry
        g = 2 * t
        mx_b, mx_c, state = list(mx_a), list(mx_a), list(state)
        for h in heads:
            mx_b[h] = scores(g + 1, h, sb_sc)
            state[h] = weights(g, h, sa_sc, mx_a[h], state[h])
        for h in heads:
            mx_c[h] = scores(g + 2, h, sa_sc)
            state[h] = weights(g + 1, h, sb_sc, mx_b[h], state[h])
        return tuple(state), tuple(mx_c)

    full_steps = qi // (2 * GB)
    state, mx_a = lax.fori_loop(0, full_steps, past_step, (tuple(state), tuple(mx_a)))

    g_tail = 2 * full_steps
    left = qi - 2 * GB * full_steps

    def tail_one(carry):
        state, mx_a = carry
        return tuple(weights(g_tail, h, sa_sc, mx_a[h], state[h]) for h in heads)

    def tail_two(carry):
        state, mx_a = carry
        mx_b, state = list(mx_a), list(state)
        for h in heads:
            mx_b[h] = scores(g_tail + 1, h, sb_sc)
            state[h] = weights(g_tail, h, sa_sc, mx_a[h], state[h])
        return tuple(weights(g_tail + 1, h, sb_sc, mx_b[h], state[h]) for h in heads)

    def tail_any(carry):
        return lax.cond(left > GB, tail_two, tail_one, carry)

    state = lax.cond(left > 0, tail_any, lambda carry: carry[0], (state, mx_a))
    out_t = jnp.concatenate([state[h][1][0:DHA] / state[h][1][DHA:DHA + 1] for h in heads], axis=0)
    out_ref[...] = out_t.T.astype(BF16)


def _moba(pa, vt):
    bsz, s, _ = pa.shape
    BS = MOBA_BLOCK
    nh = MOBA_STEP_HEADS
    w = nh * MOBA_HEAD_DIM
    nstep = MOBA_HEADS // nh
    nb = s // BS
    gb = MOBA_GROUP
    assert nb % gb == 0 and gb <= SUBLANES and w % LANES == 0
    return pl.pallas_call(
        _moba_kernel,
        out_shape=jax.ShapeDtypeStruct((bsz, s, nstep * w), BF16),
        grid=(bsz, nstep, nb),
        in_specs=[pl.BlockSpec((None, BS, w), lambda b, p, i: (b, i, p)),
                  pl.BlockSpec((None, s, w), lambda b, p, i: (b, 0, nstep + p)),
                  pl.BlockSpec((None, nb // gb, w, gb * BS), lambda b, p, i: (b, 0, p, 0))],
        out_specs=pl.BlockSpec((None, BS, w), lambda b, p, i: (b, i, p)),
        scratch_shapes=[pltpu.VMEM((nb, w), F32), pltpu.VMEM((nh, nb + SUBLANES, BS), F32),
                        pltpu.VMEM((nh, gb, BS, BS), F32), pltpu.VMEM((nh, gb, BS, BS), F32)],
        compiler_params=pltpu.CompilerParams(dimension_semantics=("arbitrary",) * 3,
                                             vmem_limit_bytes=VMEM_LIMIT),
        name="moba",
    )(pa, pa, vt)


def _outproj_kernel(hm_ref, ha_ref, x_ref, gate_ref, shift_ref, scale_ref, g_ref, wo_ref, wr_ref, br_ref,
                    x1_ref, hn_ref, ri_ref, cnt_ref):
    dm = hm_ref.shape[1]
    mix = _dot(hm_ref[...], wo_ref[0:dm, :]) + _dot(ha_ref[...], wo_ref[dm:, :])
    x1 = x_ref[...] + gate_ref[...] * mix
    x1_ref[...] = x1
    ms = jnp.mean(x1 * x1, axis=-1, keepdims=True)
    hn = x1 * lax.rsqrt(ms + EPS) * g_ref[...] * (1.0 + scale_ref[...]) + shift_ref[...]
    _store_rows(hn_ref, hn)
    logits = _dot(hn.astype(BF16), wr_ref[...]) + br_ref[...]

    tm = logits.shape[0]
    lane = _iota((tm, LANES), 1)
    lanef = lane.astype(F32)
    big = float(LANES)
    in_groups = lane < N_GROUPS
    gl = jnp.where(in_groups, logits, -jnp.inf)
    gmax = jnp.max(gl, axis=-1, keepdims=True)
    gidx = jnp.min(jnp.where(gl == gmax, lanef, big), axis=-1, keepdims=True)
    g_w = 1.0 / jnp.sum(jnp.where(in_groups, jnp.exp(gl - gmax), 0.0), axis=-1, keepdims=True)
    lo = N_GROUPS + EXPERTS_PER_GROUP * gidx
    el = jnp.where(jnp.logical_and(lanef >= lo, lanef < lo + EXPERTS_PER_GROUP), logits, -jnp.inf)
    e1 = jnp.max(el, axis=-1, keepdims=True)
    i1 = jnp.min(jnp.where(el == e1, lanef, big), axis=-1, keepdims=True)
    el = jnp.where(lanef == i1, -jnp.inf, el)
    e2 = jnp.max(el, axis=-1, keepdims=True)
    i2 = jnp.min(jnp.where(el == e2, lanef, big), axis=-1, keepdims=True)
    t = jnp.exp(e2 - e1)
    w1 = g_w / (1.0 + t)
    w2 = g_w * t / (1.0 + t)
    ri = jnp.where(lane == 0, i1 - N_GROUPS,
                   jnp.where(lane == 1, i2 - N_GROUPS,
                             jnp.where(lane == 2, w1, jnp.where(lane == 3, w2, 0.0))))
    ri_ref[...] = ri

    @pl.when(jnp.logical_and(pl.program_id(0) == 0, pl.program_id(1) == 0))
    def _():
        cnt_ref[...] = jnp.zeros_like(cnt_ref)
    picked = jnp.logical_or(lanef == i1 - N_GROUPS, lanef == i2 - N_GROUPS)
    cnt_ref[...] = cnt_ref[...] + jnp.sum(jnp.where(picked, 1.0, 0.0), axis=0, keepdims=True)


def _outproj(hm, ha, x, mod, g2, wo, wr, br):
    bsz, s, d = x.shape
    tm = min(TOKEN_TILE, s)
    dh = hm.shape[2]
    const = lambda b, i: (0, 0)
    tile = lambda w: pl.BlockSpec((None, tm, w), lambda b, i: (b, i, 0))
    modrow = lambda k: pl.BlockSpec((None, None, 1, d), lambda b, i: (b, k, 0, 0))
    return pl.pallas_call(
        _outproj_kernel,
        out_shape=(jax.ShapeDtypeStruct((bsz, s, d), F32),
                   jax.ShapeDtypeStruct((bsz, s * SUBLANES, LANES), F32),
                   jax.ShapeDtypeStruct((bsz, s, LANES), F32),
                   jax.ShapeDtypeStruct((SUBLANES, LANES), F32)),
        grid=(bsz, s // tm),
        in_specs=[tile(dh), tile(dh), tile(d), modrow(2), modrow(3), modrow(4),
                  pl.BlockSpec((1, d), const),
                  pl.BlockSpec((2 * dh, d), const),
                  pl.BlockSpec((d, LANES), const),
                  pl.BlockSpec((1, LANES), const)],
        out_specs=(tile(d), pl.BlockSpec((None, tm * SUBLANES, LANES), lambda b, i: (b, i, 0)), tile(LANES),
                   pl.BlockSpec((SUBLANES, LANES), const)),
        compiler_params=pltpu.CompilerParams(dimension_semantics=("arbitrary", "arbitrary"),
                                             vmem_limit_bytes=VMEM_LIMIT),
        name="outproj",
    )(hm, ha, x, mod, mod, mod, g2, wo, wr, br)


def _rank_kernel(ri_ref, cnt_ref, dest_ref, run_sc):
    tm = ri_ref.shape[0]
    lane = _iota((tm, LANES), 1).astype(F32)
    ri = ri_ref[...]
    oh0 = lane == ri[:, 0:1]
    oh1 = lane == ri[:, 1:2]
    oh = jnp.where(jnp.logical_or(oh0, oh1), 1.0, 0.0)

    @pl.when(pl.program_id(0) == 0)
    def _():
        padded = jnp.ceil(cnt_ref[...] / ROW_BLOCK) * ROW_BLOCK
        l8 = _iota((SUBLANES, LANES), 1)
        acc = padded
        k = 1
        while k < LANES:
            acc = acc + jnp.where(l8 >= k, pltpu.roll(acc, k, 1), 0.0)
            k *= 2
        run_sc[...] = acc - padded

    row = _iota((tm, tm), 0)
    col = _iota((tm, tm), 1)
    before = _dot((col < row).astype(BF16), oh.astype(BF16))
    tot = before + run_sc[0:1, :]
    d0 = jnp.sum(jnp.where(oh0, tot, 0.0), axis=-1, keepdims=True)
    d1 = jnp.sum(jnp.where(oh1, tot, 0.0), axis=-1, keepdims=True)
    lane_i = _iota((tm, LANES), 1)
    dest_ref[...] = jnp.where(lane_i == 0, d0, jnp.where(lane_i == 1, d1, 0.0))
    run_sc[...] = run_sc[...] + jnp.sum(oh, axis=0, keepdims=True)


def _rank(ri, cnt):
    t = ri.shape[0]
    tm = min(TOKEN_TILE, t)
    return pl.pallas_call(
        _rank_kernel,
        out_shape=jax.ShapeDtypeStruct((t, LANES), F32),
        grid=(t // tm,),
        in_specs=[pl.BlockSpec((tm, LANES), lambda i: (i, 0)),
                  pl.BlockSpec((SUBLANES, LANES), lambda i: (0, 0))],
        out_specs=pl.BlockSpec((tm, LANES), lambda i: (i, 0)),
        scratch_shapes=[pltpu.VMEM((SUBLANES, LANES), F32)],
        compiler_params=pltpu.CompilerParams(dimension_semantics=("arbitrary",)),
        name="rank",
    )(ri, cnt)


def _dispatch_kernel(pe_ref, nu_ref, dest_ref, hn_ref, xpad_ref, zbuf, sem, zsem):
    tm = hn_ref.shape[0] // SUBLANES
    block_rows = ROW_BLOCK * SUBLANES
    nblk = xpad_ref.shape[0] // block_rows

    def zero_block(j):
        rows = pl.ds(pl.multiple_of(j * block_rows, block_rows), block_rows)
        return pltpu.make_async_copy(zbuf, xpad_ref.at[rows, :], zsem)

    @pl.when(pl.program_id(0) == 0)
    def _():
        zbuf[...] = jnp.zeros_like(zbuf)

        def tail(e, n):
            first = jnp.where(e == 0, 0, pe_ref[jnp.maximum(e - 1, 0)])
            nonempty = pe_ref[e] > first

            @pl.when(nonempty)
            def _():
                zero_block(pe_ref[e] - 1).start()
            return n + nonempty.astype(jnp.int32)
        started = lax.fori_loop(0, N_EXPERTS, tail, 0)

        def unused(j, c):
            zero_block(j).start()
            return c
        lax.fori_loop(nu_ref[0], nblk, unused, 0)

        def drain(j, c):
            zero_block(0).wait()
            return c
        lax.fori_loop(0, started + nblk - nu_ref[0], drain, 0)

    def row_copy(t, k):
        d = dest_ref[0, 2 * t + k]
        return pltpu.make_async_copy(_token_tile(hn_ref, t), _token_tile(xpad_ref, d), sem)

    def issue(t, carry):
        row_copy(t, 0).start(priority=0)
        row_copy(t, 1).start(priority=1)
        return carry
    lax.fori_loop(0, tm, issue, 0, unroll=ROW_DMA_UNROLL)

    for _ in range(2):
        pltpu.make_async_copy(hn_ref, xpad_ref.at[pl.ds(0, tm * SUBLANES), :], sem).wait()


def _dispatch(pad_ends, n_used, dest, hn, rows):
    t = hn.shape[0] // SUBLANES
    tm = min(DISPATCH_TILE, t)
    return pl.pallas_call(
        _dispatch_kernel,
        out_shape=jax.ShapeDtypeStruct((rows * SUBLANES, LANES), hn.dtype),
        grid_spec=pltpu.PrefetchScalarGridSpec(
            num_scalar_prefetch=2,
            grid=(t // tm,),
            in_specs=[pl.BlockSpec((None, 1, 2 * tm), lambda i, pe, nu: (i, 0, 0), memory_space=pltpu.SMEM),
                      pl.BlockSpec((tm * SUBLANES, LANES), lambda i, pe, nu: (i, 0))],
            out_specs=pl.BlockSpec(memory_space=pl.ANY),
            scratch_shapes=[pltpu.VMEM((ROW_BLOCK * SUBLANES, LANES), hn.dtype),
                            pltpu.SemaphoreType.DMA, pltpu.SemaphoreType.DMA]),
        compiler_params=pltpu.CompilerParams(dimension_semantics=("arbitrary",)),
        name="dispatch",
    )(pad_ends, n_used, dest.reshape(t // tm, 1, 2 * tm), hn)


def _expert_kernel(be_ref, nu_ref, x_ref, wg_ref, wu_ref, wd_ref, y_ref, wgb, wub, wdb):
    i = pl.program_id(0)
    live = i < nu_ref[0]
    new_expert = jnp.logical_or(i == 0, be_ref[i] != be_ref[jnp.maximum(i - 1, 0)])

    @pl.when(jnp.logical_and(live, new_expert))
    def _():
        wgb[...] = wg_ref[...].astype(BF16)
        wub[...] = wu_ref[...].astype(BF16)
        wdb[...] = wd_ref[...].astype(BF16)

    @pl.when(live)
    def _():
        xb = _load_rows(x_ref).astype(BF16)
        g = _dot(xb, wgb[...])
        u = _dot(xb, wub[...])
        _store_rows(y_ref, _dot((g * _sigmoid(g) * u).astype(BF16), wdb[...]))

    @pl.when(jnp.logical_not(live))
    def _():
        y_ref[...] = jnp.zeros_like(y_ref)


def _experts(blk_expert, n_used, xpad, wg, wu, wd):
    rows = xpad.shape[0] // SUBLANES
    d, de = wg.shape[1], wg.shape[2]
    assert d == SUBLANES * LANES
    row_block = (ROW_BLOCK * SUBLANES, LANES)
    nblk = rows // ROW_BLOCK
    live = lambda i, be, nu: jnp.minimum(i, nu[0] - 1)
    return pl.pallas_call(
        _expert_kernel,
        out_shape=jax.ShapeDtypeStruct(xpad.shape, F32),
        grid_spec=pltpu.PrefetchScalarGridSpec(
            num_scalar_prefetch=2,
            grid=(nblk,),
            in_specs=[pl.BlockSpec(row_block, lambda i, be, nu: (live(i, be, nu), 0)),
                      pl.BlockSpec((None, d, de), lambda i, be, nu: (be[live(i, be, nu)], 0, 0)),
                      pl.BlockSpec((None, d, de), lambda i, be, nu: (be[live(i, be, nu)], 0, 0)),
                      pl.BlockSpec((None, de, d), lambda i, be, nu: (be[live(i, be, nu)], 0, 0))],
            out_specs=pl.BlockSpec(row_block, lambda i, be, nu: (i, 0)),
            scratch_shapes=[pltpu.VMEM((d, de), BF16), pltpu.VMEM((d, de), BF16), pltpu.VMEM((de, d), BF16)]),
        compiler_params=pltpu.CompilerParams(dimension_semantics=("arbitrary",),
                                             vmem_limit_bytes=VMEM_LIMIT),
        name="experts",
    )(blk_expert, n_used, xpad, wg, wu, wd)


def _combine_kernel(final_norm, dest_ref, dnext_ref, ypad_ref, x1_ref, ri_ref, gate_ref, gf_ref, out_ref, ybuf, sem):
    tm = x1_ref.shape[0]
    step = pl.program_id(0) * pl.num_programs(1) + pl.program_id(1)
    nsteps = pl.num_programs(0) * pl.num_programs(1)
    slot = step % 2

    def gather(idx_ref, buf):
        def row_copy(t, k):
            d = idx_ref[0, 2 * t + k]
            return pltpu.make_async_copy(_token_tile(ypad_ref, d), _token_tile(ybuf.at[buf, k], t), sem.at[buf])

        def issue(t, carry):
            row_copy(t, 0).start(priority=0)
            row_copy(t, 1).start(priority=1)
            return carry
        lax.fori_loop(0, tm, issue, 0, unroll=ROW_DMA_UNROLL)

    @pl.when(step == 0)
    def _():
        gather(dest_ref, 0)

    @pl.when(step + 1 < nsteps)
    def _():
        gather(dnext_ref, 1 - slot)

    for k in range(2):
        pltpu.make_async_copy(ypad_ref.at[pl.ds(0, tm * SUBLANES), :], ybuf.at[slot, k], sem.at[slot]).wait()

    ri = ri_ref[...]
    ffn = ri[:, 2:3] * _load_rows(ybuf.at[slot, 0]) + ri[:, 3:4] * _load_rows(ybuf.at[slot, 1])
    x2 = x1_ref[...] + gate_ref[...] * ffn
    if final_norm:
        ms = jnp.mean(x2 * x2, axis=-1, keepdims=True)
        x2 = x2 * lax.rsqrt(ms + EPS) * gf_ref[...]
    out_ref[...] = x2


def _combine(dest, ypad, x1, ri, mod, gf, final_norm):
    bsz, s, d = x1.shape
    tm = min(COMBINE_TILE, s)
    nt = s // tm
    dest3 = dest.reshape(bsz * nt, 1, 2 * tm)
    return pl.pallas_call(
        functools.partial(_combine_kernel, final_norm),
        out_shape=jax.ShapeDtypeStruct((bsz, s, d), F32),
        grid=(bsz, nt),
        in_specs=[pl.BlockSpec((None, 1, 2 * tm), lambda b, i: (b * nt + i, 0, 0), memory_space=pltpu.SMEM),
                  pl.BlockSpec((None, 1, 2 * tm), lambda b, i: (jnp.minimum(b * nt + i + 1, bsz * nt - 1), 0, 0),
                               memory_space=pltpu.SMEM),
                  pl.BlockSpec(memory_space=pl.ANY),
                  pl.BlockSpec((None, tm, d), lambda b, i: (b, i, 0)),
                  pl.BlockSpec((None, tm, LANES), lambda b, i: (b, i, 0)),
                  pl.BlockSpec((None, None, 1, d), lambda b, i: (b, 5, 0, 0)),
                  pl.BlockSpec((1, d), lambda b, i: (0, 0))],
        out_specs=pl.BlockSpec((None, tm, d), lambda b, i: (b, i, 0)),
        scratch_shapes=[pltpu.VMEM((2, 2, tm * SUBLANES, LANES), F32), pltpu.SemaphoreType.DMA((2,))],
        compiler_params=pltpu.CompilerParams(dimension_semantics=("arbitrary", "arbitrary"),
                                             vmem_limit_bytes=VMEM_LIMIT),
        name="combine",
    )(dest3, dest3, ypad, x1, ri.reshape(bsz, s, LANES), mod, gf)


def _pad_lanes(a):
    return jnp.pad(a, ((0, 0), (0, LANES - a.shape[1])))


def kernel(x, c, w_ada, b_ada, g_norm1, w_in, w_conv, b_conv, b_gates, g_mlstm_head, w_out, g_norm2,
           w_router_group, b_router_group, w_router_expert, b_router_expert, w_expert_gate, w_expert_up,
           w_expert_down, g_final):
    bsz, s, d = x.shape
    t = bsz * s
    dm = MLSTM_HEADS * MLSTM_HEAD_DIM
    da = MOBA_HEADS * MOBA_HEAD_DIM
    ng = 2 * MLSTM_HEADS
    n_assign = 2 * t
    n_blocks = -(-n_assign // ROW_BLOCK) + N_EXPERTS
    rows = n_blocks * ROW_BLOCK

    for l in range(w_ada.shape[0]):
        mod = _adaln(c, w_ada[l], b_ada[l]).reshape(bsz, 6, 1, d)

        w = w_in[l]
        wm = w[:, :4 * dm].astype(BF16)
        wgates = w[:, 4 * dm:4 * dm + ng]
        wa = w[:, 4 * dm + ng:4 * dm + ng + 2 * da].astype(BF16)
        wvt = w[:, 4 * dm + ng + 2 * da:].T.astype(BF16)
        pm, pa, vt, gc, gt = _inproj(x, mod, g_norm1[l].reshape(1, d), wm, wa, wvt,
                                     _pad_lanes(wgates).astype(BF16), wgates.T.astype(BF16),
                                     _pad_lanes(b_gates[l].reshape(1, ng)), b_gates[l].reshape(ng, 1))
        hm = _mlstm(pm, gc, gt, w_conv[l], b_conv[l], g_mlstm_head[l])
        ha = _moba(pa, vt)

        wr = _pad_lanes(jnp.concatenate([w_router_group[l], w_router_expert[l]], axis=1)).astype(BF16)
        br = _pad_lanes(jnp.concatenate([b_router_group[l], b_router_expert[l]]).reshape(1, -1))
        x1, hn2, ri, cnt = _outproj(hm, ha, x, mod, g_norm2[l].reshape(1, d), w_out[l].astype(BF16), wr, br)
        ri = ri.reshape(t, LANES)
        destf = _rank(ri, cnt)
        dest = destf[:, :2].astype(jnp.int32).reshape(n_assign)
        counts = cnt[0, :N_EXPERTS].astype(jnp.int32)
        pad_ends = jnp.cumsum((counts + ROW_BLOCK - 1) // ROW_BLOCK)
        blk_ids = jnp.arange(n_blocks, dtype=jnp.int32)
        blk_expert = jnp.minimum(jnp.sum((pad_ends[None, :] <= blk_ids[:, None]).astype(jnp.int32), axis=1),
                                 N_EXPERTS - 1)
        n_used = pad_ends[-1:].astype(jnp.int32)
        xpad = _dispatch(pad_ends.astype(jnp.int32), n_used, dest, hn2.reshape(t * SUBLANES, LANES), rows)
        ypad = _experts(blk_expert, n_used, xpad, w_expert_gate[l], w_expert_up[l], w_expert_down[l])
        x = _combine(dest, ypad, x1, ri, mod, g_final.reshape(1, d), l == w_ada.shape[0] - 1)
    return x
```
